```python
import math
import jax, jax.numpy as jnp
from jax import lax
import numpy as np

D_MODEL = 1024
BATCH = 8
SEQ = 4096
DEPTH = 2
DEC_BATCH = 32
DEC_SEQ = 1
PAST_LEN = 16384
PAGE_SIZE = 128

N_MIXERS = 2
N_ATTN = (DEPTH + 1) // 2
N_CONV = DEPTH // 2
N_HEADS = 16
HEAD_DIM = D_MODEL // N_HEADS
BLOCK = 256
TOPK = 3
QUERY_CHUNK = 16
CONV_WIDTH = 3
D_FF = 4 * D_MODEL
NORM_EPS = 1e-5
F32 = jnp.float32

kernel_name = "moba_shortconv_hybrid_step"


def rms_norm(x, g):
    x32 = x.astype(F32)
    y = x32 * lax.rsqrt(jnp.mean(x32 * x32, axis=-1, keepdims=True) + NORM_EPS)
    return (y * g.astype(F32)).astype(x.dtype)


def sqrelu_mlp(h, w1, w2):
    return jnp.square(jax.nn.relu(h @ w1)) @ w2


def alibi_slopes():
    return jnp.exp2(-8.0 * jnp.arange(1, N_HEADS + 1, dtype=F32) / N_HEADS)


def split_qkv(h, w_qkv):
    b, t, _ = h.shape
    qkv = (h @ w_qkv).reshape(b, t, 3, N_HEADS, HEAD_DIM).transpose(2, 0, 3, 1, 4)
    return qkv[0], qkv[1], qkv[2]


def select_blocks(q, kmean, t):
    nb = kmean.shape[2]
    n_past = t // BLOCK
    gate = jnp.einsum('bhqd,bhnd->bhqn', q.astype(F32), kmean)
    gate = jnp.where(jnp.arange(nb)[None, :] < n_past[:, None], gate, -jnp.inf)
    if nb < TOPK:
        gate = jnp.pad(gate, ((0, 0), (0, 0), (0, 0), (0, TOPK - nb)), constant_values=-jnp.inf)
    _, idx = lax.top_k(gate, TOPK)
    idx = jnp.minimum(idx, nb - 1).astype(jnp.int32)
    own = jnp.broadcast_to((t // BLOCK).astype(jnp.int32)[:, None], idx.shape[:3] + (1,))
    blk = jnp.concatenate([idx, own], axis=-1)
    slot_ok = jnp.concatenate([jnp.arange(TOPK)[None, :] < n_past[:, None],
                               jnp.ones((t.shape[0], 1), bool)], axis=-1)
    return blk, slot_ok


def block_attend(q, t, blk, slot_ok, k_cand, v_cand, slopes):
    s = blk[..., None] * BLOCK + jnp.arange(BLOCK)
    dist = t[:, None, None] - s
    ok = slot_ok[:, :, None] & (dist >= 0)
    logits = jnp.einsum('bhqd,bhqnkd->bhqnk', q, k_cand).astype(F32) * (HEAD_DIM ** -0.5)
    logits = logits - slopes[None, :, None, None, None] * dist.astype(F32)
    logits = jnp.where(ok, logits, -jnp.inf)
    b, h, nq, ns, bl = logits.shape
    p = jax.nn.softmax(logits.reshape(b, h, nq, ns * bl), axis=-1).reshape(b, h, nq, ns, bl)
    return jnp.einsum('bhqnk,bhqnkd->bhqd', p.astype(v_cand.dtype), v_cand)


def moba_prompt(h, w_qkv, slopes):
    b, s_len, _ = h.shape
    q, k, v = split_qkv(h, w_qkv)
    nb = -(-s_len // BLOCK)
    pad = ((0, 0), (0, 0), (0, nb * BLOCK - s_len), (0, 0))
    kblk = jnp.pad(k, pad).reshape(b, N_HEADS, nb, BLOCK, HEAD_DIM)
    vblk = jnp.pad(v, pad).reshape(b, N_HEADS, nb, BLOCK, HEAD_DIM)
    kmean = jnp.sum(kblk, axis=3, dtype=F32) / BLOCK
    t = jnp.arange(s_len, dtype=jnp.int32)
    blk, slot_ok = select_blocks(q, kmean, t)
    b_ix = jnp.arange(b)[:, None, None, None]
    h_ix = jnp.arange(N_HEADS)[None, :, None, None]

    def chunk(c):
        t0 = c * QUERY_CHUNK
        qc = lax.dynamic_slice_in_dim(q, t0, QUERY_CHUNK, axis=2)
        bc = lax.dynamic_slice_in_dim(blk, t0, QUERY_CHUNK, axis=2)
        okc = lax.dynamic_slice_in_dim(slot_ok, t0, QUERY_CHUNK, axis=0)
        tc = t0 + jnp.arange(QUERY_CHUNK, dtype=jnp.int32)
        kc = kblk[b_ix, h_ix, bc]
        vc = vblk[b_ix, h_ix, bc]
        return block_attend(qc, tc, bc, okc, kc, vc, slopes)

    out = lax.map(chunk, jnp.arange(s_len // QUERY_CHUNK))
    out = out.transpose(1, 0, 3, 2, 4).reshape(b, s_len, D_MODEL)
    return out, k, v


def moba_sample(h, cache_k_l, cache_v_l, page_table, w_qkv, slopes):
    bd, t_len, _ = h.shape
    q, k, v = split_qkv(h, w_qkv)
    n_pages = page_table.shape[1]
    ppb = BLOCK // PAGE_SIZE
    n_cblk = PAST_LEN // BLOCK
    ntp = (PAST_LEN % BLOCK) // PAGE_SIZE
    tail_len = ntp * PAGE_SIZE + t_len
    n_tblk = -(-tail_len // BLOCK)
    t = PAST_LEN + jnp.arange(t_len, dtype=jnp.int32)

    def tail_blocks(cache_l, new):
        pt = page_table[:, n_cblk * ppb: n_cblk * ppb + ntp]
        old = cache_l[pt].transpose(0, 2, 1, 3, 4).reshape(bd, N_HEADS, ntp * PAGE_SIZE, HEAD_DIM)
        tail = jnp.concatenate([old.astype(new.dtype), new], axis=2)
        tail = jnp.pad(tail, ((0, 0), (0, 0), (0, n_tblk * BLOCK - tail_len), (0, 0)))
        return tail.reshape(bd, N_HEADS, n_tblk, BLOCK, HEAD_DIM)

    ktail = tail_blocks(cache_k_l, k)
    vtail = tail_blocks(cache_v_l, v)
    page_sum = jnp.sum(cache_k_l, axis=2, dtype=F32)
    csum = page_sum[page_table[:, :n_cblk * ppb]].reshape(bd, n_cblk, ppb, N_HEADS, HEAD_DIM).sum(2)
    cmean = (csum / BLOCK).transpose(0, 2, 1, 3)
    tmean = jnp.sum(ktail, axis=3, dtype=F32) / BLOCK
    kmean = jnp.concatenate([cmean, tmean], axis=2)
    blk, slot_ok = select_blocks(q, kmean, t)

    bd5 = jnp.arange(bd)[:, None, None, None, None]
    h5 = jnp.arange(N_HEADS)[None, :, None, None, None]
    lp = jnp.clip(blk[..., None] * ppb + jnp.arange(ppb), 0, n_pages - 1)
    phys = page_table[bd5, lp]
    tb = jnp.clip(blk - n_cblk, 0, n_tblk - 1)
    bd4 = jnp.arange(bd)[:, None, None, None]
    h4 = jnp.arange(N_HEADS)[None, :, None, None]
    from_cache = (blk < n_cblk)[..., None, None]
    cshape = blk.shape + (BLOCK, HEAD_DIM)
    k_cand = jnp.where(from_cache, cache_k_l[phys, h5].reshape(cshape).astype(k.dtype), ktail[bd4, h4, tb])
    v_cand = jnp.where(from_cache, cache_v_l[phys, h5].reshape(cshape).astype(v.dtype), vtail[bd4, h4, tb])
    out = block_attend(q, t, blk, slot_ok, k_cand, v_cand, slopes)
    out = out.transpose(0, 2, 1, 3).reshape(bd, t_len, D_MODEL)
    return out, k, v


def short_conv_mixer(h, prev, w_in, conv_w, w_out):
    bg, cg, xin = jnp.split(h @ w_in, 3, axis=-1)
    u = cg * xin
    t_len = u.shape[1]
    uf = jnp.concatenate([prev.astype(u.dtype), u], axis=1)
    conv = sum(conv_w[j] * uf[:, j:j + t_len] for j in range(CONV_WIDTH))
    return (bg * conv) @ w_out, uf[:, -(CONV_WIDTH - 1):]


def setup_inputs(seed: int = 0) -> dict:
    key = jax.random.key(seed)
    ks = jax.random.split(key, 20)
    n_pages = PAST_LEN // PAGE_SIZE
    n_used = DEC_BATCH * n_pages
    n_phys = n_used + n_used // 4
    nrm = jax.random.normal
    page_table = jax.random.permutation(ks[0], n_phys)[:n_used].reshape(DEC_BATCH, n_pages).astype(jnp.int32)
    return {
        "x_prompt": nrm(ks[1], (BATCH, SEQ, D_MODEL), F32),
        "x_sample": nrm(ks[2], (DEC_BATCH, DEC_SEQ, D_MODEL), F32),
        "cache_k": nrm(ks[3], (N_ATTN, n_phys, N_HEADS, PAGE_SIZE, HEAD_DIM), F32),
        "cache_v": nrm(ks[4], (N_ATTN, n_phys, N_HEADS, PAGE_SIZE, HEAD_DIM), F32),
        "state_conv": nrm(ks[5], (N_CONV, DEC_BATCH, CONV_WIDTH - 1, D_MODEL), F32),
        "page_table": page_table,
        "norm_mix": 1.0 + 0.05 * nrm(ks[6], (DEPTH, D_MODEL), F32),
        "norm_mlp": 1.0 + 0.05 * nrm(ks[7], (DEPTH, D_MODEL), F32),
        "norm_final": 1.0 + 0.05 * nrm(ks[8], (D_MODEL,), F32),
        "w_qkv": nrm(ks[9], (N_ATTN, D_MODEL, 3 * D_MODEL), F32) * D_MODEL ** -0.5,
        "w_attn_out": nrm(ks[10], (N_ATTN, D_MODEL, D_MODEL), F32) * D_MODEL ** -0.5,
        "w_conv_in": nrm(ks[11], (N_CONV, D_MODEL, 3 * D_MODEL), F32) * D_MODEL ** -0.5,
        "conv_w": nrm(ks[12], (N_CONV, CONV_WIDTH, D_MODEL), F32) * CONV_WIDTH ** -0.5,
        "w_conv_out": nrm(ks[13], (N_CONV, D_MODEL, D_MODEL), F32) * D_MODEL ** -0.5,
        "w_mlp_in": nrm(ks[14], (DEPTH, D_MODEL, D_FF), F32) * D_MODEL ** -0.5,
        "w_mlp_out": nrm(ks[15], (DEPTH, D_FF, D_MODEL), F32) * D_FF ** -0.5,
    }


def reference(x_prompt, x_sample, cache_k, cache_v, state_conv, page_table,
              norm_mix, norm_mlp, norm_final, w_qkv, w_attn_out,
              w_conv_in, conv_w, w_conv_out, w_mlp_in, w_mlp_out):
    slopes = alibi_slopes()
    yp, ys = x_prompt, x_sample
    kp_l, vp_l, ks_l, vs_l, cp_l, cs_l = [], [], [], [], [], []
    for i in range(DEPTH):
        hp = rms_norm(yp, norm_mix[i])
        hs = rms_norm(ys, norm_mix[i])
        j = i // N_MIXERS
        if i % N_MIXERS == 0:
            op, kp, vp = moba_prompt(hp, w_qkv[j], slopes)
            os_, ks_, vs_ = moba_sample(hs, cache_k[j], cache_v[j], page_table, w_qkv[j], slopes)
            yp = yp + op @ w_attn_out[j]
            ys = ys + os_ @ w_attn_out[j]
            kp_l.append(kp); vp_l.append(vp); ks_l.append(ks_); vs_l.append(vs_)
        else:
            zero_prev = jnp.zeros((hp.shape[0], CONV_WIDTH - 1, D_MODEL), hp.dtype)
            op, cp = short_conv_mixer(hp, zero_prev, w_conv_in[j], conv_w[j], w_conv_out[j])
            os_, cs = short_conv_mixer(hs, state_conv[j], w_conv_in[j], conv_w[j], w_conv_out[j])
            yp = yp + op
            ys = ys + os_
            cp_l.append(cp); cs_l.append(cs)
        yp = yp + sqrelu_mlp(rms_norm(yp, norm_mlp[i]), w_mlp_in[i], w_mlp_out[i])
        ys = ys + sqrelu_mlp(rms_norm(ys, norm_mlp[i]), w_mlp_in[i], w_mlp_out[i])
    y_prompt = rms_norm(yp, norm_final)
    y_sample = rms_norm(ys, norm_final)
    k_prompt = jnp.stack(kp_l)
    v_prompt = jnp.stack(vp_l)
    k_sample = jnp.stack(ks_l)
    v_sample = jnp.stack(vs_l)
    conv_prompt = jnp.stack(cp_l)
    conv_sample = jnp.stack(cs_l)
    return (y_prompt, y_sample, k_prompt, v_prompt, k_sample, v_sample, conv_prompt, conv_sample)
```

```python
import functools

import jax
import jax.numpy as jnp
from jax import lax
from jax.experimental import pallas as pl
from jax.experimental.pallas import tpu as pltpu

D_MODEL = 1024
N_HEADS = 16
HEAD_DIM = D_MODEL // N_HEADS
BLOCK = 256
TOPK = 3
PAGE_SIZE = 128
PAGES_PER_BLOCK = BLOCK // PAGE_SIZE
CONV_WIDTH = 3
D_FF = 4 * D_MODEL
NORM_EPS = 1e-5
QK_SCALE = HEAD_DIM ** -0.5

F32 = jnp.float32
BF16 = jnp.bfloat16

HEADS_PER_STEP = 2
PAIR_WIDTH = HEADS_PER_STEP * HEAD_DIM
TOKEN_TILE = 256
SAMPLE_ROWS = 8
BLOCKS_PER_SUM_STEP = 8
VMEM_LIMIT_BYTES = 56 * 1024 * 1024

_NT = (((1,), (1,)), ((), ()))


def _dot(a, b):
    return jnp.dot(a, b, preferred_element_type=F32)


def _dot_nt(a, b):
    return lax.dot_general(a, b, _NT, preferred_element_type=F32)


def _rms_norm(x, g):
    return (x * lax.rsqrt(jnp.mean(x * x, axis=-1, keepdims=True) + NORM_EPS)) * g


def _sqrelu_mlp(y, g, w1_ref, w2_ref):
    hn = _rms_norm(y, g).astype(BF16)
    hid = jnp.maximum(_dot(hn, w1_ref[...]), 0.0)
    return _dot((hid * hid).astype(BF16), w2_ref[...])


def _params(*semantics):
    return pltpu.CompilerParams(dimension_semantics=semantics,
                                vmem_limit_bytes=VMEM_LIMIT_BYTES)


def _resident(shape):
    zeros = (0,) * len(shape)
    return pl.BlockSpec(shape, lambda *_: zeros, pipeline_mode=pl.Buffered(1))


def _prompt_qkv_kernel(x_ref, g_ref, wq_t_ref, wk_ref, wk_t_ref, wv_t_ref,
                       q_t_ref, k_ref, k_t_ref, v_t_ref, v_t_bf_ref, ksum_ref):
    hn = _rms_norm(x_ref[0], g_ref[...]).astype(BF16)
    t = hn.shape[0]
    q_t_ref[0] = _dot_nt(wq_t_ref[...], hn).astype(BF16)
    k = _dot(hn, wk_ref[...])
    k_ref[0] = k.astype(BF16)
    for j in range(t // BLOCK):
        ksum_ref[0, j] = jnp.sum(k[j * BLOCK:(j + 1) * BLOCK], axis=0, keepdims=True)
    k_t_ref[0] = _dot_nt(wk_t_ref[...], hn).reshape(N_HEADS, HEAD_DIM, t)
    v_t = _dot_nt(wv_t_ref[...], hn)
    v_t_ref[0] = v_t.reshape(N_HEADS, HEAD_DIM, t)
    for j in range(t // BLOCK):
        v_t_bf_ref[0, j] = v_t[:, j * BLOCK:(j + 1) * BLOCK].astype(BF16)


def _prompt_qkv(x, g, wq_t, wk, wk_t, wv_t):
    b, s, d = x.shape
    t = TOKEN_TILE
    nb = s // BLOCK
    nbt = t // BLOCK
    return pl.pallas_call(
        _prompt_qkv_kernel,
        grid=(b, s // t),
        in_specs=[
            pl.BlockSpec((1, t, d), lambda i, j: (i, j, 0)),
            _resident((1, d)),
            _resident((d, d)), _resident((d, d)), _resident((d, d)), _resident((d, d)),
        ],
        out_specs=[
            pl.BlockSpec((1, d, t), lambda i, j: (i, 0, j)),
            pl.BlockSpec((1, t, d), lambda i, j: (i, j, 0)),
            pl.BlockSpec((1, N_HEADS, HEAD_DIM, t), lambda i, j: (i, 0, 0, j)),
            pl.BlockSpec((1, N_HEADS, HEAD_DIM, t), lambda i, j: (i, 0, 0, j)),
            pl.BlockSpec((1, nbt, d, BLOCK), lambda i, j: (i, j, 0, 0)),
            pl.BlockSpec((1, nbt, 1, d), lambda i, j: (i, j, 0, 0)),
        ],
        out_shape=[
            jax.ShapeDtypeStruct((b, d, s), BF16),
            jax.ShapeDtypeStruct((b, s, d), BF16),
            jax.ShapeDtypeStruct((b, N_HEADS, HEAD_DIM, s), F32),
            jax.ShapeDtypeStruct((b, N_HEADS, HEAD_DIM, s), F32),
            jax.ShapeDtypeStruct((b, nb, d, BLOCK), BF16),
            jax.ShapeDtypeStruct((b, nb, 1, d), F32),
        ],
        compiler_params=_params("parallel", "parallel"),
        name="prompt_qkv",
    )(x, g, wq_t, wk, wk_t, wv_t)


def _select_top_blocks(gate, n_valid):
    nidx = lax.broadcasted_iota(jnp.int32, gate.shape, 0)
    valid = nidx < n_valid
    g = jnp.where(valid, gate, -jnp.inf)
    sel = jnp.zeros(gate.shape, F32)
    for _ in range(TOPK):
        mx = jnp.max(g, axis=0, keepdims=True)
        first = jnp.min(jnp.where(g == mx, nidx, gate.shape[0]), axis=0, keepdims=True)
        pick = (nidx == first) & valid
        sel = jnp.where(pick, 1.0, sel)
        g = jnp.where(pick, -jnp.inf, g)
    return sel


def _prompt_attn_kernel(slope_ref, q_t_ref, k_ref, v_t_ref, ksum_ref, o_ref, sel_ref):
    c = pl.program_id(2)
    q_t = q_t_ref[0]
    row = lax.broadcasted_iota(jnp.int32, q_t.shape, 0)
    kmean = (ksum_ref[0] * (1.0 / BLOCK)).astype(BF16)
    ki = lax.broadcasted_iota(jnp.int32, (BLOCK, BLOCK), 0)
    qj = lax.broadcasted_iota(jnp.int32, (BLOCK, BLOCK), 1)
    dist0 = (qj - ki).astype(F32)
    causal = ki <= qj

    outs = []
    for hh in range(HEADS_PER_STEP):
        lo = hh * HEAD_DIM
        slope = slope_ref[0, hh:hh + 1, :]
        q_h = jnp.where((row >= lo) & (row < lo + HEAD_DIM), q_t, jnp.zeros_like(q_t))
        sel_ref[hh] = _select_top_blocks(_dot(kmean, q_h), c)
        bias0 = slope * dist0

        k_own = k_ref[0, pl.ds(pl.multiple_of(c * BLOCK, BLOCK), BLOCK), :]
        z = jnp.where(causal, _dot(k_own, q_h) - bias0, -jnp.inf)
        m = jnp.max(z, axis=0, keepdims=True)
        p = jnp.exp(z - m)
        l = jnp.sum(p, axis=0, keepdims=True)
        acc = _dot(v_t_ref[0, c, lo:lo + HEAD_DIM, :], p.astype(BF16))

        def past_block(n, carry, hh=hh, lo=lo, slope=slope, q_h=q_h, bias0=bias0):
            m, l, acc = carry
            k_n = k_ref[0, pl.ds(pl.multiple_of(n * BLOCK, BLOCK), BLOCK), :]
            z = _dot(k_n, q_h) - bias0
            off = slope * jnp.full(slope.shape, (c - n) * BLOCK, jnp.int32).astype(F32)
            m_blk = jnp.max(z, axis=0, keepdims=True) - off
            chosen = sel_ref[hh, pl.ds(n, 1), :] > 0.5
            m_new = jnp.where(chosen, jnp.maximum(m, m_blk), m)
            alpha = jnp.exp(m - m_new)
            p = jnp.exp(z - jnp.where(chosen, m_new + off, jnp.inf))
            l = alpha * l + jnp.sum(p, axis=0, keepdims=True)
            acc = alpha * acc + _dot(v_t_ref[0, n, lo:lo + HEAD_DIM, :], p.astype(BF16))
            return m_new, l, acc

        m, l, acc = lax.fori_loop(0, c, past_block, (m, l, acc))
        outs.append(acc / l)
    o_ref[0] = jnp.concatenate(outs, axis=0).T.astype(BF16)


def _prompt_attn(slopes, q_t, k, v_t_blk, ksum):
    b, s, d = k.shape
    nb = s // BLOCK
    return pl.pallas_call(
        _prompt_attn_kernel,
        grid=(b, d // PAIR_WIDTH, nb),
        in_specs=[
            pl.BlockSpec((1, HEADS_PER_STEP, BLOCK), lambda i, h, c: (h, 0, 0)),
            pl.BlockSpec((1, PAIR_WIDTH, BLOCK), lambda i, h, c: (i, h, c)),
            pl.BlockSpec((1, s, PAIR_WIDTH), lambda i, h, c: (i, 0, h)),
            pl.BlockSpec((1, nb, PAIR_WIDTH, BLOCK), lambda i, h, c: (i, 0, h, 0)),
            pl.BlockSpec((1, nb, PAIR_WIDTH), lambda i, h, c: (i, 0, h)),
        ],
        out_specs=pl.BlockSpec((1, BLOCK, PAIR_WIDTH), lambda i, h, c: (i, c, h)),
        out_shape=jax.ShapeDtypeStruct((b, s, d), BF16),
        scratch_shapes=[pltpu.VMEM((HEADS_PER_STEP, nb, BLOCK), F32)],
        compiler_params=_params("parallel", "parallel", "arbitrary"),
        name="prompt_attn",
    )(slopes, q_t, k, v_t_blk, ksum)


def _attn_out_mlp_kernel(o_ref, x_ref, wo_ref, g_ref, w1_ref, w2_ref, y_ref):
    y = x_ref[...] + _dot(o_ref[...], wo_ref[...])
    y_ref[...] = y + _sqrelu_mlp(y, g_ref[...], w1_ref, w2_ref)


def _attn_out_mlp(o, x, wo, g, w1, w2, tile):
    m, d = x.shape
    return pl.pallas_call(
        _attn_out_mlp_kernel,
        grid=(m // tile,),
        in_specs=[
            pl.BlockSpec((tile, d), lambda i: (i, 0)),
            pl.BlockSpec((tile, d), lambda i: (i, 0)),
            _resident((d, d)), _resident((1, d)),
            _resident((d, D_FF)), _resident((D_FF, d)),
        ],
        out_specs=pl.BlockSpec((tile, d), lambda i: (i, 0)),
        out_shape=jax.ShapeDtypeStruct((m, d), F32),
        compiler_params=_params("parallel"),
        name="attn_out_mlp",
    )(o, x, wo, g, w1, w2)


def _conv_layer_tail(y, bg, conv, wcout_ref, g_mlp, w1_ref, w2_ref, g_final):
    y = y + _dot((bg * conv).astype(BF16), wcout_ref[...])
    y = y + _sqrelu_mlp(y, g_mlp, w1_ref, w2_ref)
    return _rms_norm(y, g_final)


def _gates(y, g_mix, wcin_ref):
    d = y.shape[-1]
    z = _dot(_rms_norm(y, g_mix).astype(BF16), wcin_ref[...])
    return z[:, :d], z[:, d:2 * d] * z[:, 2 * d:]


def _prompt_conv_mlp_kernel(y_ref, g_mix_ref, wcin_ref, cw_ref, wcout_ref, g_mlp_ref,
                            w1_ref, w2_ref, g_fin_ref, out_ref, state_ref, carry_ref):
    j = pl.program_id(1)

    @pl.when(j == 0)
    def _():
        carry_ref[...] = jnp.zeros_like(carry_ref)

    y = y_ref[0]
    t = y.shape[0]
    bg, u = _gates(y, g_mix_ref[...], wcin_ref)
    r = lax.broadcasted_iota(jnp.int32, u.shape, 0)
    prev2 = carry_ref[0:1, :]
    prev1 = carry_ref[1:2, :]
    u1 = jnp.where(r == 0, prev1, pltpu.roll(u, 1, 0))
    u2 = jnp.where(r == 0, prev2, jnp.where(r == 1, prev1, pltpu.roll(u, 2, 0)))
    conv = cw_ref[0:1, :] * u2 + cw_ref[1:2, :] * u1 + cw_ref[2:3, :] * u
    carry_ref[...] = u[t - (CONV_WIDTH - 1):, :]
    out_ref[0] = _conv_layer_tail(y, bg, conv, wcout_ref, g_mlp_ref[...],
                                  w1_ref, w2_ref, g_fin_ref[...])

    @pl.when(j == pl.num_programs(1) - 1)
    def _():
        state_ref[0] = u[t - (CONV_WIDTH - 1):, :]


def _prompt_conv_mlp(y, g_mix, wcin, cw, wcout, g_mlp, w1, w2, g_fin):
    b, s, d = y.shape
    t = TOKEN_TILE
    return pl.pallas_call(
        _prompt_conv_mlp_kernel,
        grid=(b, s // t),
        in_specs=[
            pl.BlockSpec((1, t, d), lambda i, j: (i, j, 0)),
            _resident((1, d)), _resident((d, 3 * d)), _resident((CONV_WIDTH, d)),
            _resident((d, d)), _resident((1, d)),
            _resident((d, D_FF)), _resident((D_FF, d)), _resident((1, d)),
        ],
        out_specs=[
            pl.BlockSpec((1, t, d), lambda i, j: (i, j, 0)),
            pl.BlockSpec((1, CONV_WIDTH - 1, d), lambda i, j: (i, 0, 0)),
        ],
        out_shape=[
            jax.ShapeDtypeStruct((b, s, d), F32),
            jax.ShapeDtypeStruct((b, CONV_WIDTH - 1, d), F32),
        ],
        scratch_shapes=[pltpu.VMEM((CONV_WIDTH - 1, d), F32)],
        compiler_params=_params("parallel", "arbitrary"),
        name="prompt_conv_mlp",
    )(y, g_mix, wcin, cw, wcout, g_mlp, w1, w2, g_fin)


def _sample_conv_mlp_kernel(y_ref, prev2_ref, prev1_ref, g_mix_ref, wcin_ref, cw_ref,
                            wcout_ref, g_mlp_ref, w1_ref, w2_ref, g_fin_ref, out_ref, u_ref):
    y = y_ref[...]
    bg, u = _gates(y, g_mix_ref[...], wcin_ref)
    conv = cw_ref[0:1, :] * prev2_ref[...] + cw_ref[1:2, :] * prev1_ref[...] + cw_ref[2:3, :] * u
    u_ref[...] = u
    out_ref[...] = _conv_layer_tail(y, bg, conv, wcout_ref, g_mlp_ref[...],
                                    w1_ref, w2_ref, g_fin_ref[...])


def _sample_conv_mlp(y, prev2, prev1, g_mix, wcin, cw, wcout, g_mlp, w1, w2, g_fin):
    m, d = y.shape
    return pl.pallas_call(
        _sample_conv_mlp_kernel,
        grid=(1,),
        in_specs=[
            _resident((m, d)), _resident((m, d)), _resident((m, d)),
            _resident((1, d)), _resident((d, 3 * d)), _resident((CONV_WIDTH, d)),
            _resident((d, d)), _resident((1, d)),
            _resident((d, D_FF)), _resident((D_FF, d)), _resident((1, d)),
        ],
        out_specs=[pl.BlockSpec((m, d), lambda i: (0, 0)), pl.BlockSpec((m, d), lambda i: (0, 0))],
        out_shape=[jax.ShapeDtypeStruct((m, d), F32), jax.ShapeDtypeStruct((m, d), F32)],
        compiler_params=_params("arbitrary"),
        name="sample_conv_mlp",
    )(y, prev2, prev1, g_mix, wcin, cw, wcout, g_mlp, w1, w2, g_fin)


def _sample_qkv_kernel(x_ref, g_ref, w_ref, qkv_ref):
    qkv_ref[...] = _dot(_rms_norm(x_ref[...], g_ref[...]).astype(BF16), w_ref[...])


def _sample_qkv(x, g, wqkv):
    m, d = x.shape
    return pl.pallas_call(
        _sample_qkv_kernel,
        grid=(1,),
        in_specs=[_resident((m, d)), _resident((1, d)), _resident((d, 3 * d))],
        out_specs=pl.BlockSpec((m, 3 * d), lambda i: (0, 0)),
        out_shape=jax.ShapeDtypeStruct((m, 3 * d), F32),
        compiler_params=_params("arbitrary"),
        name="sample_qkv",
    )(x, g, wqkv)


def _cache_block_sum_kernel(pt_ref, *refs):
    page_refs, out_ref = refs[:-1], refs[-1]
    d = out_ref.shape[-1]
    rows = lax.broadcasted_iota(jnp.int32, (BLOCKS_PER_SUM_STEP, PAGE_SIZE), 0)
    acc = jnp.zeros((BLOCKS_PER_SUM_STEP, d), F32)
    for j in range(BLOCKS_PER_SUM_STEP):
        blk = page_refs[PAGES_PER_BLOCK * j][0, 0].reshape(d, PAGE_SIZE)
        for i in range(1, PAGES_PER_BLOCK):
            blk = blk + page_refs[PAGES_PER_BLOCK * j + i][0, 0].reshape(d, PAGE_SIZE)
        hi = blk.astype(BF16)
        lo = (blk - hi.astype(F32)).astype(BF16)
        onehot = jnp.where(rows == j, 1.0, 0.0).astype(BF16)
        acc = acc + _dot_nt(onehot, hi) + _dot_nt(onehot, lo)
    out_ref[0] = acc


def _cache_block_sums(page_table, cache_t):
    nseq, n_pages = page_table.shape
    _, _, nh, hd, ps = cache_t.shape
    pages_per_step = BLOCKS_PER_SUM_STEP * PAGES_PER_BLOCK
    n_blocks = n_pages // PAGES_PER_BLOCK

    def page_spec(i):
        return pl.BlockSpec(
            (1, 1, nh, hd, ps),
            lambda b, j, pt: (0, pt[b, j * pages_per_step + i], 0, 0, 0))

    return pl.pallas_call(
        _cache_block_sum_kernel,
        grid_spec=pltpu.PrefetchScalarGridSpec(
            num_scalar_prefetch=1,
            grid=(nseq, n_pages // pages_per_step),
            in_specs=[page_spec(i) for i in range(pages_per_step)],
            out_specs=pl.BlockSpec((1, BLOCKS_PER_SUM_STEP, nh * hd), lambda b, j, pt: (b, j, 0)),
        ),
        out_shape=jax.ShapeDtypeStruct((nseq, n_blocks, nh * hd), F32),
        compiler_params=_params("parallel", "arbitrary"),
        name="cache_block_sums",
    )(page_table, *([cache_t] * pages_per_step))


def _sample_gate_kernel(q_ref, csum_ref, blk_ref):
    q = q_ref[0]
    d = q.shape[-1]
    head_row = lax.broadcasted_iota(jnp.int32, (N_HEADS, d), 0)
    head_col = lax.broadcasted_iota(jnp.int32, (N_HEADS, d), 1) // HEAD_DIM
    q_heads = jnp.where(head_row == head_col, q, 0.0).astype(BF16)
    kmean = (csum_ref[0] * (1.0 / BLOCK)).astype(BF16)
    g = _dot_nt(q_heads, kmean)
    nb = g.shape[-1]
    lane = lax.broadcasted_iota(jnp.int32, g.shape, 1)
    out_lane = lax.broadcasted_iota(jnp.int32, blk_ref.shape[1:], 1)
    out = jnp.zeros(blk_ref.shape[1:], jnp.int32)
    for r in range(TOPK):
        mx = jnp.max(g, axis=1, keepdims=True)
        first = jnp.min(jnp.where(g == mx, lane, nb), axis=1, keepdims=True)
        out = jnp.where(out_lane == r, first, out)
        g = jnp.where(lane == first, -jnp.inf, g)
    blk_ref[0] = out


def _sample_gate(q, csum):
    nseq, nb, d = csum.shape
    return pl.pallas_call(
        _sample_gate_kernel,
        grid=(nseq,),
        in_specs=[
            pl.BlockSpec((1, 1, d), lambda b: (b, 0, 0)),
            pl.BlockSpec((1, nb, d), lambda b: (b, 0, 0)),
        ],
        out_specs=pl.BlockSpec((1, N_HEADS, PAGE_SIZE), lambda b: (b, 0, 0)),
        out_shape=jax.ShapeDtypeStruct((nseq, N_HEADS, PAGE_SIZE), jnp.int32),
        compiler_params=_params("parallel"),
        name="sample_gate",
    )(q, csum)


def _sample_attn_kernel(past_len, pt_ref, blk_ref, slope_ref, q_ref, knew_ref, vnew_ref, *refs):
    n_pages = TOPK * PAGES_PER_BLOCK
    k_refs, v_refs, o_ref = refs[:n_pages], refs[n_pages:2 * n_pages], refs[-1]
    b = pl.program_id(0)
    h = pl.program_id(1)
    q = q_ref[0, 0]
    slope = slope_ref[0]
    lane = lax.broadcasted_iota(jnp.int32, (SAMPLE_ROWS, PAGE_SIZE), 1)

    logits = []
    for s in range(TOPK):
        blk = blk_ref[b, h * TOPK + s]
        for i in range(PAGES_PER_BLOCK):
            k_t = k_refs[s * PAGES_PER_BLOCK + i][0, 0, 0].astype(BF16)
            pos = blk * BLOCK + i * PAGE_SIZE + lane
            logits.append(_dot(q, k_t) - slope * (past_len - pos).astype(F32))
    k_new = knew_ref[0, 0].astype(F32)
    self_logit = jnp.sum(q.astype(F32) * k_new, axis=-1, keepdims=True)

    m = self_logit
    for z in logits:
        m = jnp.maximum(m, jnp.max(z, axis=-1, keepdims=True))
    p_self = jnp.exp(self_logit - m)
    l = p_self
    acc = p_self.astype(BF16).astype(F32) * vnew_ref[0, 0].astype(F32)
    for z, v_ref in zip(logits, v_refs):
        p = jnp.exp(z - m)
        l = l + jnp.sum(p, axis=-1, keepdims=True)
        acc = acc + _dot_nt(p.astype(BF16), v_ref[0, 0, 0].astype(BF16))
    o_ref[0, 0] = acc / l


def _sample_attn(page_table, blk, slopes, q, k_new, v_new, cache_k_t, cache_v_t, past_len):
    nseq = page_table.shape[0]
    _, _, nh, hd, ps = cache_k_t.shape

    def page_spec(s, i):
        return pl.BlockSpec(
            (1, 1, 1, hd, ps),
            lambda b, h, pt, bk: (0, pt[b, bk[b, h * TOPK + s] * PAGES_PER_BLOCK + i], h, 0, 0))

    page_specs = [page_spec(s, i) for s in range(TOPK) for i in range(PAGES_PER_BLOCK)]
    row_spec = pl.BlockSpec((1, 1, SAMPLE_ROWS, hd), lambda b, h, pt, bk: (b, h, 0, 0))
    n_pages = len(page_specs)
    return pl.pallas_call(
        functools.partial(_sample_attn_kernel, past_len),
        grid_spec=pltpu.PrefetchScalarGridSpec(
            num_scalar_prefetch=2,
            grid=(nseq, nh),
            in_specs=[pl.BlockSpec((1, 1, ps), lambda b, h, pt, bk: (h, 0, 0)),
                      row_spec, row_spec, row_spec] + page_specs + page_specs,
            out_specs=row_spec,
        ),
        out_shape=jax.ShapeDtypeStruct((nseq, nh, SAMPLE_ROWS, hd), F32),
        compiler_params=_params("parallel", "arbitrary"),
        name="sample_attn",
    )(page_table, blk, slopes, q, k_new, v_new,
      *([cache_k_t] * n_pages), *([cache_v_t] * n_pages))


def kernel(x_prompt, x_sample, cache_k, cache_v, state_conv, page_table, norm_mix, norm_mlp,
           norm_final, w_qkv, w_attn_out, w_conv_in, conv_w, w_conv_out, w_mlp_in, w_mlp_out):
    b, s, d = x_prompt.shape
    nseq, dec_seq, _ = x_sample.shape
    n_pages = page_table.shape[1]
    past_len = n_pages * PAGE_SIZE
    assert d == D_MODEL and s % TOKEN_TILE == 0 and TOKEN_TILE % BLOCK == 0
    assert dec_seq == 1 and past_len % BLOCK == 0, "the new token must open a fresh key block"
    assert n_pages % (BLOCKS_PER_SUM_STEP * PAGES_PER_BLOCK) == 0
    assert w_qkv.shape[0] == 1 and w_conv_in.shape[0] == 1 and w_mlp_in.shape[0] == 2

    wqkv = w_qkv[0]
    wq = wqkv[:, :d] * QK_SCALE
    wq_t = wq.T.astype(BF16)
    wk = wqkv[:, d:2 * d].astype(BF16)
    wk_t = wk.T
    wv_t = wqkv[:, 2 * d:].T.astype(BF16)
    wo = w_attn_out[0].astype(BF16)
    wcin = w_conv_in[0].astype(BF16)
    wcout = w_conv_out[0].astype(BF16)
    w1 = w_mlp_in.astype(BF16)
    w2 = w_mlp_out.astype(BF16)
    g_mix = norm_mix.reshape(2, 1, d)
    g_mlp = norm_mlp.reshape(2, 1, d)
    g_fin = norm_final.reshape(1, d)
    cw = conv_w[0]

    slopes = jnp.exp2(-8.0 * jnp.arange(1, N_HEADS + 1, dtype=F32) / N_HEADS)
    slopes_pair = jnp.broadcast_to(
        slopes.reshape(N_HEADS // HEADS_PER_STEP, HEADS_PER_STEP, 1),
        (N_HEADS // HEADS_PER_STEP, HEADS_PER_STEP, BLOCK))
    slopes_head = jnp.broadcast_to(slopes.reshape(N_HEADS, 1, 1), (N_HEADS, 1, PAGE_SIZE))

    q_t, k_bf, k_t, v_t, v_t_blk, ksum = _prompt_qkv(x_prompt, g_mix[0], wq_t, wk, wk_t, wv_t)
    attn = _prompt_attn(slopes_pair, q_t, k_bf, v_t_blk, ksum.reshape(b, s // BLOCK, d))
    y = _attn_out_mlp(attn.reshape(b * s, d), x_prompt.reshape(b * s, d),
                      wo, g_mlp[0], w1[0], w2[0], TOKEN_TILE)
    y_prompt, conv_prompt = _prompt_conv_mlp(y.reshape(b, s, d), g_mix[1], wcin, cw, wcout,
                                             g_mlp[1], w1[1], w2[1], g_fin)

    xs = x_sample.reshape(nseq, d)
    qkv = _sample_qkv(xs, g_mix[0], jnp.concatenate([wq.astype(BF16), wk, wv_t.T], axis=1))
    q_s, k_s, v_s = qkv[:, :d], qkv[:, d:2 * d], qkv[:, 2 * d:]
    cache_k_t = jnp.swapaxes(cache_k, -1, -2)
    cache_v_t = jnp.swapaxes(cache_v, -1, -2)
    csum = _cache_block_sums(page_table, cache_k_t)
    blk = _sample_gate(q_s.reshape(nseq, 1, d), csum)[:, :, :TOPK].reshape(nseq, N_HEADS * TOPK)

    def rows(a, dtype):
        a = a.reshape(nseq, N_HEADS, 1, HEAD_DIM).astype(dtype)
        return jnp.broadcast_to(a, (nseq, N_HEADS, SAMPLE_ROWS, HEAD_DIM))

    o_s = _sample_attn(page_table, blk, slopes_head, rows(q_s, BF16), rows(k_s, BF16),
                       rows(v_s, BF16), cache_k_t, cache_v_t, past_len)
    o_s = o_s[:, :, 0, :].reshape(nseq, d).astype(BF16)
    ys = _attn_out_mlp(o_s, xs, wo, g_mlp[0], w1[0], w2[0], nseq)
    prev = state_conv[0]
    y_sample, u_s = _sample_conv_mlp(ys, prev[:, 0], prev[:, 1], g_mix[1], wcin, cw, wcout,
                                     g_mlp[1], w1[1], w2[1], g_fin)

    k_prompt = jnp.swapaxes(k_t, -1, -2)[None]
    v_prompt = jnp.swapaxes(v_t, -1, -2)[None]
    k_sample = k_s.reshape(1, nseq, N_HEADS, 1, HEAD_DIM)
    v_sample = v_s.reshape(1, nseq, N_HEADS, 1, HEAD_DIM)
    conv_sample = jnp.stack([prev[:, 1], u_s], axis=1)[None]
    return (y_prompt, y_sample.reshape(nseq, 1, d), k_prompt, v_prompt, k_sample, v_sample,
            conv_prompt[None], conv_sample)
```

```python
import functools

import jax
import jax.numpy as jnp
from jax import lax
from jax.experimental import pallas as pl
from jax.experimental.pallas import tpu as pltpu

D_MODEL = 1024
N_HEADS = 16
HEAD_DIM = D_MODEL // N_HEADS
BLOCK = 256
TOPK = 3
PAGE_SIZE = 128
PAGES_PER_BLOCK = BLOCK // PAGE_SIZE
CONV_WIDTH = 3
D_FF = 4 * D_MODEL
NORM_EPS = 1e-5
QK_SCALE = HEAD_DIM ** -0.5
LOG2_E = 1.4426950408889634

F32 = jnp.float32
BF16 = jnp.bfloat16

HEADS_PER_STEP = 4
ROUTE_ROWS = 24
QK_WIDTH = 128
BIAS_TERMS = 3
V_ROWS = 80
TOKEN_TILE = 256
SAMPLE_ROWS = 8
SAMPLE_HEADS_PER_STEP = 4
BLOCKS_PER_SUM_STEP = 8
VMEM_LIMIT_BYTES = 56 * 1024 * 1024

_NT = (((1,), (1,)), ((), ()))


def _dot(a, b):
    return jnp.dot(a, b, preferred_element_type=F32)


def _dot_nt(a, b):
    return lax.dot_general(a, b, _NT, preferred_element_type=F32)


def _rms_norm(x, g):
    return (x * lax.rsqrt(jnp.mean(x * x, axis=-1, keepdims=True) + NORM_EPS)) * g


def _sqrelu_mlp(y, g, w1_ref, w2_ref):
    hn = _rms_norm(y, g).astype(BF16)
    hid = jnp.maximum(_dot(hn, w1_ref[...]), 0.0)
    return _dot((hid * hid).astype(BF16), w2_ref[...])


def _params(*semantics):
    return pltpu.CompilerParams(dimension_semantics=semantics,
                                vmem_limit_bytes=VMEM_LIMIT_BYTES)


def _resident(shape):
    zeros = (0,) * len(shape)
    return pl.BlockSpec(shape, lambda *_: zeros, pipeline_mode=pl.Buffered(1))


def _prompt_qkv_kernel(x_ref, g_ref, wq_t_ref, wk_ref, wk_t_ref, wv_t_ref, kbias_ref,
                       q_t_ref, k_ref, k_t_ref, v_t_ref, v_t_bf_ref, ksum_ref):
    hn = _rms_norm(x_ref[0], g_ref[...]).astype(BF16)
    t = hn.shape[0]
    q_t = _dot_nt(wq_t_ref[...], hn)
    q_pad = QK_WIDTH - HEAD_DIM
    ones_rows = jnp.where(lax.broadcasted_iota(jnp.int32, (q_pad, t), 0) < BIAS_TERMS,
                          1.0, 0.0).astype(BF16)
    for h in range(N_HEADS):
        q_t_ref[0, h * QK_WIDTH:h * QK_WIDTH + HEAD_DIM, :] = (
            q_t[h * HEAD_DIM:(h + 1) * HEAD_DIM].astype(BF16))
        q_t_ref[0, h * QK_WIDTH + HEAD_DIM:(h + 1) * QK_WIDTH, :] = ones_rows

    k = _dot(hn, wk_ref[...])
    for j in range(t // BLOCK):
        k_j = k[j * BLOCK:(j + 1) * BLOCK]
        ksum_ref[0, j] = jnp.sum(k_j, axis=0, keepdims=True)
        k_ref[0, j * BLOCK:(j + 1) * BLOCK, :] = (k_j + kbias_ref[...]).astype(BF16)
    k_t_ref[0] = _dot_nt(wk_t_ref[...], hn).reshape(N_HEADS, HEAD_DIM, t)

    v_t = _dot_nt(wv_t_ref[...], hn)
    v_t_ref[0] = v_t.reshape(N_HEADS, HEAD_DIM, t)
    v_pad = V_ROWS - HEAD_DIM
    ones_row = jnp.where(lax.broadcasted_iota(jnp.int32, (v_pad, BLOCK), 0) == 0,
                         1.0, 0.0).astype(BF16)
    for j in range(t // BLOCK):
        for h in range(N_HEADS):
            v_t_bf_ref[0, j, h * V_ROWS:h * V_ROWS + HEAD_DIM, :] = (
                v_t[h * HEAD_DIM:(h + 1) * HEAD_DIM, j * BLOCK:(j + 1) * BLOCK].astype(BF16))
            v_t_bf_ref[0, j, h * V_ROWS + HEAD_DIM:(h + 1) * V_ROWS, :] = ones_row


def _prompt_qkv(x, g, wq_t, wk_wide, wk_t, wv_t, kbias):
    b, s, d = x.shape
    t = TOKEN_TILE
    nb = s // BLOCK
    nbt = t // BLOCK
    dq = N_HEADS * QK_WIDTH
    dv = N_HEADS * V_ROWS
    return pl.pallas_call(
        _prompt_qkv_kernel,
        grid=(b, s // t),
        in_specs=[
            pl.BlockSpec((1, t, d), lambda i, j: (i, j, 0)),
            _resident((1, d)),
            _resident((d, d)), _resident((d, dq)), _resident((d, d)), _resident((d, d)),
            _resident((BLOCK, dq)),
        ],
        out_specs=[
            pl.BlockSpec((1, dq, t), lambda i, j: (i, 0, j)),
            pl.BlockSpec((1, t, dq), lambda i, j: (i, j, 0)),
            pl.BlockSpec((1, N_HEADS, HEAD_DIM, t), lambda i, j: (i, 0, 0, j)),
            pl.BlockSpec((1, N_HEADS, HEAD_DIM, t), lambda i, j: (i, 0, 0, j)),
            pl.BlockSpec((1, nbt, dv, BLOCK), lambda i, j: (i, j, 0, 0)),
            pl.BlockSpec((1, nbt, 1, dq), lambda i, j: (i, j, 0, 0)),
        ],
        out_shape=[
            jax.ShapeDtypeStruct((b, dq, s), BF16),
            jax.ShapeDtypeStruct((b, s, dq), BF16),
            jax.ShapeDtypeStruct((b, N_HEADS, HEAD_DIM, s), F32),
            jax.ShapeDtypeStruct((b, N_HEADS, HEAD_DIM, s), F32),
            jax.ShapeDtypeStruct((b, nb, dv, BLOCK), BF16),
            jax.ShapeDtypeStruct((b, nb, 1, dq), F32),
        ],
        compiler_params=_params("parallel", "parallel"),
        name="prompt_qkv",
    )(x, g, wq_t, wk_wide, wk_t, wv_t, kbias)


def _select_routed(gate, c):
    pidx = lax.broadcasted_iota(jnp.int32, gate.shape, 0)
    valid = (pidx >= 1) & (pidx <= c)
    g = jnp.where(valid, gate, -jnp.inf)
    sel = jnp.where(pidx == 0, 1.0, 0.0)
    for _ in range(TOPK):
        mx = jnp.max(g, axis=0, keepdims=True)
        first = jnp.min(jnp.where(g == mx, pidx, gate.shape[0]), axis=0, keepdims=True)
        pick = (pidx == first) & valid
        sel = jnp.where(pick, 1.0, sel)
        g = jnp.where(pick, -jnp.inf, g)
    return sel


def _prompt_attn_kernel(slope_ref, q_t_ref, k_ref, v_t_ref, ksum_ref, o_ref,
                        sel_ref, m_ref, acc_ref, alpha_ref, z_ref, p_ref):
    c = pl.program_id(2)
    nb = v_t_ref.shape[1]
    heads = range(HEADS_PER_STEP)
    q = [q_t_ref[0, hh * QK_WIDTH:(hh + 1) * QK_WIDTH, :] for hh in heads]
    slope = [slope_ref[0, hh:hh + 1, :] for hh in heads]

    def block_of(pos):
        if isinstance(pos, int) and pos == 0:
            return c
        return jnp.where(pos == 0, c, jnp.clip(pos - 1, 0, nb - 1))

    def scores(pos, slot, bias=None):
        n = block_of(pos)
        rows = pl.ds(pl.multiple_of(n * BLOCK, BLOCK), BLOCK)
        for hh in heads:
            z = _dot(k_ref[0, rows, hh * QK_WIDTH:(hh + 1) * QK_WIDTH], q[hh])
            z_ref[slot, hh] = z if bias is None else z + bias

    def softmax(pos, slot):
        n = block_of(pos)
        for hh in heads:
            z = z_ref[slot, hh]
            off = slope[hh] * jnp.full(slope[hh].shape, (c - n) * BLOCK, jnp.int32).astype(F32)
            routed = sel_ref[hh, pl.ds(pos, 1), :] > 0.5
            m_old = m_ref[hh]
            m_new = jnp.where(routed, jnp.maximum(m_old, jnp.max(z, axis=0, keepdims=True) - off), m_old)
            m_ref[hh] = m_new
            alpha_ref[slot, hh] = jnp.exp2(m_old - m_new)
            p_ref[slot, hh] = jnp.exp2(z - jnp.where(routed, m_new + off, jnp.inf)).astype(BF16)

    def accumulate(pos, slot):
        n = block_of(pos)
        for hh in heads:
            pv = _dot(v_t_ref[0, n, hh * V_ROWS:(hh + 1) * V_ROWS, :], p_ref[slot, hh])
            acc_ref[hh] = alpha_ref[slot, hh] * acc_ref[hh] + pv

    kmean = (ksum_ref[0] * (1.0 / BLOCK)).astype(BF16)
    for hh in heads:
        sel_ref[hh] = _select_routed(_dot(kmean[:, hh * QK_WIDTH:(hh + 1) * QK_WIDTH], q[hh]), c)

    m_ref[...] = jnp.full(m_ref.shape, -1e30, F32)
    acc_ref[...] = jnp.zeros(acc_ref.shape, F32)
    alpha_ref[1] = jnp.ones(alpha_ref.shape[1:], F32)
    p_ref[1] = jnp.zeros(p_ref.shape[1:], BF16)
    ki = lax.broadcasted_iota(jnp.int32, (BLOCK, BLOCK), 0)
    qj = lax.broadcasted_iota(jnp.int32, (BLOCK, BLOCK), 1)
    scores(0, 0, bias=jnp.where(ki <= qj, 0.0, -jnp.inf))

    def two_positions(s, carry):
        pos = 2 * s
        scores(pos + 1, 1)
        softmax(pos, 0)
        accumulate(pos - 1, 1)
        scores(pos + 2, 0)
        softmax(pos + 1, 1)
        accumulate(pos, 0)
        return carry

    n_pairs = lax.shift_right_logical(c + 2, 1)
    lax.fori_loop(0, n_pairs, two_positions, 0)
    accumulate(2 * n_pairs - 1, 1)

    outs = []
    for hh in heads:
        acc = acc_ref[hh]
        outs.append(acc[:HEAD_DIM] / acc[HEAD_DIM:HEAD_DIM + 1])
    o_ref[0] = jnp.concatenate(outs, axis=0).T.astype(BF16)


def _prompt_attn(slopes, q_t, k, v_t_blk, ksum_pos):
    b, s, dq = k.shape
    nb = s // BLOCK
    n_pos = ksum_pos.shape[1]
    n_steps = N_HEADS // HEADS_PER_STEP
    wq = HEADS_PER_STEP * QK_WIDTH
    wv = HEADS_PER_STEP * V_ROWS
    wo = HEADS_PER_STEP * HEAD_DIM
    assert n_pos >= nb + 2, "softmax reads the routing row of one position past the last"
    tile = (HEADS_PER_STEP, BLOCK, BLOCK)
    return pl.pallas_call(
        _prompt_attn_kernel,
        grid=(b, n_steps, nb),
        in_specs=[
            pl.BlockSpec((1, HEADS_PER_STEP, BLOCK), lambda i, h, c: (h, 0, 0)),
            pl.BlockSpec((1, wq, BLOCK), lambda i, h, c: (i, h, c)),
            pl.BlockSpec((1, s, wq), lambda i, h, c: (i, 0, h)),
            pl.BlockSpec((1, nb, wv, BLOCK), lambda i, h, c: (i, 0, h, 0)),
            pl.BlockSpec((1, n_pos, wq), lambda i, h, c: (i, 0, h)),
        ],
        out_specs=pl.BlockSpec((1, BLOCK, wo), lambda i, h, c: (i, c, h)),
        out_shape=jax.ShapeDtypeStruct((b, s, N_HEADS * HEAD_DIM), BF16),
        scratch_shapes=[
            pltpu.VMEM((HEADS_PER_STEP, n_pos, BLOCK), F32),
            pltpu.VMEM((HEADS_PER_STEP, 1, BLOCK), F32),
            pltpu.VMEM((HEADS_PER_STEP, V_ROWS, BLOCK), F32),
            pltpu.VMEM((2, HEADS_PER_STEP, 1, BLOCK), F32),
            pltpu.VMEM((2,) + tile, F32),
            pltpu.VMEM((2,) + tile, BF16),
        ],
        compiler_params=_params("parallel", "parallel", "arbitrary"),
        name="prompt_attn",
    )(slopes, q_t, k, v_t_blk, ksum_pos)


def _attn_out_mlp_kernel(o_ref, x_ref, wo_ref, g_ref, w1_ref, w2_ref, y_ref):
    y = x_ref[...] + _dot(o_ref[...], wo_ref[...])
    y_ref[...] = y + _sqrelu_mlp(y, g_ref[...], w1_ref, w2_ref)


def _attn_out_mlp(o, x, wo, g, w1, w2, tile):
    m, d = x.shape
    return pl.pallas_call(
        _attn_out_mlp_kernel,
        grid=(m // tile,),
        in_specs=[
            pl.BlockSpec((tile, d), lambda i: (i, 0)),
            pl.BlockSpec((tile, d), lambda i: (i, 0)),
            _resident((d, d)), _resident((1, d)),
            _resident((d, D_FF)), _resident((D_FF, d)),
        ],
        out_specs=pl.BlockSpec((tile, d), lambda i: (i, 0)),
        out_shape=jax.ShapeDtypeStruct((m, d), F32),
        compiler_params=_params("parallel"),
        name="attn_out_mlp",
    )(o, x, wo, g, w1, w2)


def _conv_layer_tail(y, bg, conv, wcout_ref, g_mlp, w1_ref, w2_ref, g_final):
    y = y + _dot((bg * conv).astype(BF16), wcout_ref[...])
    y = y + _sqrelu_mlp(y, g_mlp, w1_ref, w2_ref)
    return _rms_norm(y, g_final)


def _gates(y, g_mix, wcin_ref):
    d = y.shape[-1]
    z = _dot(_rms_norm(y, g_mix).astype(BF16), wcin_ref[...])
    return z[:, :d], z[:, d:2 * d] * z[:, 2 * d:]


def _prompt_conv_mlp_kernel(y_ref, g_mix_ref, wcin_ref, cw_ref, wcout_ref, g_mlp_ref,
                            w1_ref, w2_ref, g_fin_ref, out_ref, state_ref, carry_ref):
    j = pl.program_id(1)

    @pl.when(j == 0)
    def _():
        carry_ref[...] = jnp.zeros_like(carry_ref)

    y = y_ref[0]
    t = y.shape[0]
    bg, u = _gates(y, g_mix_ref[...], wcin_ref)
    r = lax.broadcasted_iota(jnp.int32, u.shape, 0)
    prev2 = carry_ref[0:1, :]
    prev1 = carry_ref[1:2, :]
    u1 = jnp.where(r == 0, prev1, pltpu.roll(u, 1, 0))
    u2 = jnp.where(r == 0, prev2, jnp.where(r == 1, prev1, pltpu.roll(u, 2, 0)))
    conv = cw_ref[0:1, :] * u2 + cw_ref[1:2, :] * u1 + cw_ref[2:3, :] * u
    carry_ref[...] = u[t - (CONV_WIDTH - 1):, :]
    out_ref[0] = _conv_layer_tail(y, bg, conv, wcout_ref, g_mlp_ref[...],
                                  w1_ref, w2_ref, g_fin_ref[...])

    @pl.when(j == pl.num_programs(1) - 1)
    def _():
        state_ref[0] = u[t - (CONV_WIDTH - 1):, :]


def _prompt_conv_mlp(y, g_mix, wcin, cw, wcout, g_mlp, w1, w2, g_fin):
    b, s, d = y.shape
    t = TOKEN_TILE
    return pl.pallas_call(
        _prompt_conv_mlp_kernel,
        grid=(b, s // t),
        in_specs=[
            pl.BlockSpec((1, t, d), lambda i, j: (i, j, 0)),
            _resident((1, d)), _resident((d, 3 * d)), _resident((CONV_WIDTH, d)),
            _resident((d, d)), _resident((1, d)),
            _resident((d, D_FF)), _resident((D_FF, d)), _resident((1, d)),
        ],
        out_specs=[
            pl.BlockSpec((1, t, d), lambda i, j: (i, j, 0)),
            pl.BlockSpec((1, CONV_WIDTH - 1, d), lambda i, j: (i, 0, 0)),
        ],
        out_shape=[
            jax.ShapeDtypeStruct((b, s, d), F32),
            jax.ShapeDtypeStruct((b, CONV_WIDTH - 1, d), F32),
        ],
        scratch_shapes=[pltpu.VMEM((CONV_WIDTH - 1, d), F32)],
        compiler_params=_params("parallel", "arbitrary"),
        name="prompt_conv_mlp",
    )(y, g_mix, wcin, cw, wcout, g_mlp, w1, w2, g_fin)


def _sample_conv_mlp_kernel(y_ref, prev2_ref, prev1_ref, g_mix_ref, wcin_ref, cw_ref,
                            wcout_ref, g_mlp_ref, w1_ref, w2_ref, g_fin_ref, out_ref, u_ref):
    y = y_ref[...]
    bg, u = _gates(y, g_mix_ref[...], wcin_ref)
    conv = cw_ref[0:1, :] * prev2_ref[...] + cw_ref[1:2, :] * prev1_ref[...] + cw_ref[2:3, :] * u
    u_ref[...] = u
    out_ref[...] = _conv_layer_tail(y, bg, conv, wcout_ref, g_mlp_ref[...],
                                    w1_ref, w2_ref, g_fin_ref[...])


def _sample_conv_mlp(y, prev2, prev1, g_mix, wcin, cw, wcout, g_mlp, w1, w2, g_fin):
    m, d = y.shape
    return pl.pallas_call(
        _sample_conv_mlp_kernel,
        grid=(1,),
        in_specs=[
            _resident((m, d)), _resident((m, d)), _resident((m, d)),
            _resident((1, d)), _resident((d, 3 * d)), _resident((CONV_WIDTH, d)),
            _resident((d, d)), _resident((1, d)),
            _resident((d, D_FF)), _resident((D_FF, d)), _resident((1, d)),
        ],
        out_specs=[pl.BlockSpec((m, d), lambda i: (0, 0)), pl.BlockSpec((m, d), lambda i: (0, 0))],
        out_shape=[jax.ShapeDtypeStruct((m, d), F32), jax.ShapeDtypeStruct((m, d), F32)],
        compiler_params=_params("arbitrary"),
        name="sample_conv_mlp",
    )(y, prev2, prev1, g_mix, wcin, cw, wcout, g_mlp, w1, w2, g_fin)


def _sample_qkv_kernel(x_ref, g_ref, w_ref, qkv_ref):
    qkv_ref[...] = _dot(_rms_norm(x_ref[...], g_ref[...]).astype(BF16), w_ref[...])


def _sample_qkv(x, g, wqkv):
    m, d = x.shape
    return pl.pallas_call(
        _sample_qkv_kernel,
        grid=(1,),
        in_specs=[_resident((m, d)), _resident((1, d)), _resident((d, 3 * d))],
        out_specs=pl.BlockSpec((m, 3 * d), lambda i: (0, 0)),
        out_shape=jax.ShapeDtypeStruct((m, 3 * d), F32),
        compiler_params=_params("arbitrary"),
        name="sample_qkv",
    )(x, g, wqkv)


def _cache_block_sum_kernel(pt_ref, *refs):
    page_refs, out_ref = refs[:-1], refs[-1]
    d = out_ref.shape[-1]
    rows = lax.broadcasted_iota(jnp.int32, (BLOCKS_PER_SUM_STEP, PAGE_SIZE), 0)
    acc = jnp.zeros((BLOCKS_PER_SUM_STEP, d), F32)
    for j in range(BLOCKS_PER_SUM_STEP):
        blk = page_refs[PAGES_PER_BLOCK * j][0, 0].reshape(d, PAGE_SIZE)
        for i in range(1, PAGES_PER_BLOCK):
            blk = blk + page_refs[PAGES_PER_BLOCK * j + i][0, 0].reshape(d, PAGE_SIZE)
        hi = blk.astype(BF16)
        lo = (blk - hi.astype(F32)).astype(BF16)
        onehot = jnp.where(rows == j, 1.0, 0.0).astype(BF16)
        acc = acc + _dot_nt(onehot, hi) + _dot_nt(onehot, lo)
    out_ref[0] = acc


def _cache_block_sums(page_table, cache_t):
    nseq, n_pages = page_table.shape
    _, _, nh, hd, ps = cache_t.shape
    pages_per_step = BLOCKS_PER_SUM_STEP * PAGES_PER_BLOCK
    n_blocks = n_pages // PAGES_PER_BLOCK

    def page_spec(i):
        return pl.BlockSpec(
            (1, 1, nh, hd, ps),
            lambda b, j, pt: (0, pt[b, j * pages_per_step + i], 0, 0, 0))

    return pl.pallas_call(
        _cache_block_sum_kernel,
        grid_spec=pltpu.PrefetchScalarGridSpec(
            num_scalar_prefetch=1,
            grid=(nseq, n_pages // pages_per_step),
            in_specs=[page_spec(i) for i in range(pages_per_step)],
            out_specs=pl.BlockSpec((1, BLOCKS_PER_SUM_STEP, nh * hd), lambda b, j, pt: (b, j, 0)),
        ),
        out_shape=jax.ShapeDtypeStruct((nseq, n_blocks, nh * hd), F32),
        compiler_params=_params("parallel", "arbitrary"),
        name="cache_block_sums",
    )(page_table, *([cache_t] * pages_per_step))


def _sample_gate_kernel(q_ref, csum_ref, blk_ref):
    q = q_ref[0]
    d = q.shape[-1]
    head_row = lax.broadcasted_iota(jnp.int32, (N_HEADS, d), 0)
    head_col = lax.broadcasted_iota(jnp.int32, (N_HEADS, d), 1) // HEAD_DIM
    q_heads = jnp.where(head_row == head_col, q, 0.0).astype(BF16)
    kmean = (csum_ref[0] * (1.0 / BLOCK)).astype(BF16)
    g = _dot_nt(q_heads, kmean)
    nb = g.shape[-1]
    lane = lax.broadcasted_iota(jnp.int32, g.shape, 1)
    out_lane = lax.broadcasted_iota(jnp.int32, blk_ref.shape[1:], 1)
    out = jnp.zeros(blk_ref.shape[1:], jnp.int32)
    for r in range(TOPK):
        mx = jnp.max(g, axis=1, keepdims=True)
        first = jnp.min(jnp.where(g == mx, lane, nb), axis=1, keepdims=True)
        out = jnp.where(out_lane == r, first, out)
        g = jnp.where(lane == first, -jnp.inf, g)
    blk_ref[0] = out


def _sample_gate(q, csum):
    nseq, nb, d = csum.shape
    return pl.pallas_call(
        _sample_gate_kernel,
        grid=(nseq,),
        in_specs=[
            pl.BlockSpec((1, 1, d), lambda b: (b, 0, 0)),
            pl.BlockSpec((1, nb, d), lambda b: (b, 0, 0)),
        ],
        out_specs=pl.BlockSpec((1, N_HEADS, PAGE_SIZE), lambda b: (b, 0, 0)),
        out_shape=jax.ShapeDtypeStruct((nseq, N_HEADS, PAGE_SIZE), jnp.int32),
        compiler_params=_params("parallel"),
        name="sample_gate",
    )(q, csum)


def _sample_attn_kernel(past_len, pt_ref, blk_ref, slope_ref, q_ref, knew_ref, vnew_ref, *refs):
    pages_per_head = TOPK * PAGES_PER_BLOCK
    n_pages = SAMPLE_HEADS_PER_STEP * pages_per_head
    k_refs, v_refs, o_ref = refs[:n_pages], refs[n_pages:2 * n_pages], refs[-1]
    b = pl.program_id(0)
    lane = lax.broadcasted_iota(jnp.int32, (SAMPLE_ROWS, PAGE_SIZE), 1)

    for hh in range(SAMPLE_HEADS_PER_STEP):
        h = pl.program_id(1) * SAMPLE_HEADS_PER_STEP + hh
        q = q_ref[0, hh]
        slope = slope_ref[hh]
        head_k = k_refs[hh * pages_per_head:(hh + 1) * pages_per_head]
        head_v = v_refs[hh * pages_per_head:(hh + 1) * pages_per_head]

        logits = []
        for s in range(TOPK):
            blk = blk_ref[b, h * TOPK + s]
            for i in range(PAGES_PER_BLOCK):
                k_t = head_k[s * PAGES_PER_BLOCK + i][0, 0, 0].astype(BF16)
                pos = blk * BLOCK + i * PAGE_SIZE + lane
                logits.append(_dot(q, k_t) - slope * (past_len - pos).astype(F32))
        k_new = knew_ref[0, hh].astype(F32)
        self_logit = jnp.sum(q.astype(F32) * k_new, axis=-1, keepdims=True)

        m = self_logit
        for z in logits:
            m = jnp.maximum(m, jnp.max(z, axis=-1, keepdims=True))
        p_self = jnp.exp(self_logit - m)
        l = p_self
        acc = p_self.astype(BF16).astype(F32) * vnew_ref[0, hh].astype(F32)
        for z, v_ref in zip(logits, head_v):
            p = jnp.exp(z - m)
            l = l + jnp.sum(p, axis=-1, keepdims=True)
            acc = acc + _dot_nt(p.astype(BF16), v_ref[0, 0, 0].astype(BF16))
        o_ref[0, hh] = acc / l


def _sample_attn(page_table, blk, slopes, q, k_new, v_new, cache_k_t, cache_v_t, past_len):
    nseq = page_table.shape[0]
    _, _, nh, hd, ps = cache_k_t.shape
    hps = SAMPLE_HEADS_PER_STEP

    def page_spec(hh, s, i):
        def index(b, g, pt, bk):
            h = g * hps + hh
            return (0, pt[b, bk[b, h * TOPK + s] * PAGES_PER_BLOCK + i], h, 0, 0)
        return pl.BlockSpec((1, 1, 1, hd, ps), index)

    page_specs = [page_spec(hh, s, i) for hh in range(hps)
                  for s in range(TOPK) for i in range(PAGES_PER_BLOCK)]
    row_spec = pl.BlockSpec((1, hps, SAMPLE_ROWS, hd), lambda b, g, pt, bk: (b, g, 0, 0))
    n_pages = len(page_specs)
    return pl.pallas_call(
        functools.partial(_sample_attn_kernel, past_len),
        grid_spec=pltpu.PrefetchScalarGridSpec(
            num_scalar_prefetch=2,
            grid=(nseq, nh // hps),
            in_specs=[pl.BlockSpec((hps, 1, ps), lambda b, g, pt, bk: (g, 0, 0)),
                      row_spec, row_spec, row_spec] + page_specs + page_specs,
            out_specs=row_spec,
        ),
        out_shape=jax.ShapeDtypeStruct((nseq, nh, SAMPLE_ROWS, hd), F32),
        compiler_params=_params("parallel", "arbitrary"),
        name="sample_attn",
    )(page_table, blk, slopes, q, k_new, v_new,
      *([cache_k_t] * n_pages), *([cache_v_t] * n_pages))


def _widen_heads(w, width):
    d = w.shape[0]
    w = w.reshape(d, N_HEADS, HEAD_DIM)
    w = jnp.pad(w, ((0, 0), (0, 0), (0, width - HEAD_DIM)))
    return w.reshape(d, N_HEADS * width)


def _alibi_key_bias(slopes):
    rest = slopes[None, :] * jnp.arange(BLOCK, dtype=F32)[:, None]
    terms = []
    for _ in range(BIAS_TERMS):
        bits = lax.bitcast_convert_type(rest, jnp.uint32) & jnp.uint32(0xFFFF0000)
        term = lax.bitcast_convert_type(bits, F32)
        terms.append(term)
        rest = rest - term
    bias = jnp.stack(terms, axis=-1)
    bias = jnp.pad(bias, ((0, 0), (0, 0), (HEAD_DIM, QK_WIDTH - HEAD_DIM - BIAS_TERMS)))
    return bias.reshape(BLOCK, N_HEADS * QK_WIDTH)


def kernel(x_prompt, x_sample, cache_k, cache_v, state_conv, page_table, norm_mix, norm_mlp,
           norm_final, w_qkv, w_attn_out, w_conv_in, conv_w, w_conv_out, w_mlp_in, w_mlp_out):
    b, s, d = x_prompt.shape
    nseq, dec_seq, _ = x_sample.shape
    n_pages = page_table.shape[1]
    past_len = n_pages * PAGE_SIZE
    assert d == D_MODEL and s % TOKEN_TILE == 0 and TOKEN_TILE % BLOCK == 0
    assert dec_seq == 1 and past_len % BLOCK == 0, "the new token must open a fresh key block"
    assert n_pages % (BLOCKS_PER_SUM_STEP * PAGES_PER_BLOCK) == 0
    assert w_qkv.shape[0] == 1 and w_conv_in.shape[0] == 1 and w_mlp_in.shape[0] == 2

    wqkv = w_qkv[0]
    wq = wqkv[:, :d] * QK_SCALE
    wq_t = (wq * LOG2_E).T.astype(BF16)
    wk = wqkv[:, d:2 * d].astype(BF16)
    wk_t = wk.T
    wv_t = wqkv[:, 2 * d:].T.astype(BF16)
    wo = w_attn_out[0].astype(BF16)
    wcin = w_conv_in[0].astype(BF16)
    wcout = w_conv_out[0].astype(BF16)
    w1 = w_mlp_in.astype(BF16)
    w2 = w_mlp_out.astype(BF16)
    g_mix = norm_mix.reshape(2, 1, d)
    g_mlp = norm_mlp.reshape(2, 1, d)
    g_fin = norm_final.reshape(1, d)
    cw = conv_w[0]

    slopes = jnp.exp2(-8.0 * jnp.arange(1, N_HEADS + 1, dtype=F32) / N_HEADS)
    slopes_log2 = slopes * LOG2_E
    slopes_step = jnp.broadcast_to(
        slopes_log2.reshape(N_HEADS // HEADS_PER_STEP, HEADS_PER_STEP, 1),
        (N_HEADS // HEADS_PER_STEP, HEADS_PER_STEP, BLOCK))
    slopes_head = jnp.broadcast_to(slopes.reshape(N_HEADS, 1, 1), (N_HEADS, 1, PAGE_SIZE))

    nb = s // BLOCK
    q_t, k_bf, k_t, v_t, v_t_blk, ksum = _prompt_qkv(
        x_prompt, g_mix[0], wq_t, _widen_heads(wk, QK_WIDTH), wk_t, wv_t,
        _alibi_key_bias(slopes_log2))
    ksum_pos = jnp.pad(ksum.reshape(b, nb, N_HEADS * QK_WIDTH),
                       ((0, 0), (1, ROUTE_ROWS - nb - 1), (0, 0)))
    attn = _prompt_attn(slopes_step, q_t, k_bf, v_t_blk, ksum_pos)
    y = _attn_out_mlp(attn.reshape(b * s, d), x_prompt.reshape(b * s, d),
                      wo, g_mlp[0], w1[0], w2[0], TOKEN_TILE)
    y_prompt, conv_prompt = _prompt_conv_mlp(y.reshape(b, s, d), g_mix[1], wcin, cw, wcout,
                                             g_mlp[1], w1[1], w2[1], g_fin)

    xs = x_sample.reshape(nseq, d)
    qkv = _sample_qkv(xs, g_mix[0], jnp.concatenate([wq.astype(BF16), wk, wv_t.T], axis=1))
    q_s, k_s, v_s = qkv[:, :d], qkv[:, d:2 * d], qkv[:, 2 * d:]
    cache_k_t = jnp.swapaxes(cache_k, -1, -2)
    cache_v_t = jnp.swapaxes(cache_v, -1, -2)
    csum = _cache_block_sums(page_table, cache_k_t)
    blk = _sample_gate(q_s.reshape(nseq, 1, d), csum)[:, :, :TOPK].reshape(nseq, N_HEADS * TOPK)

    def rows(a, dtype):
        a = a.reshape(nseq, N_HEADS, 1, HEAD_DIM).astype(dtype)
        return jnp.broadcast_to(a, (nseq, N_HEADS, SAMPLE_ROWS, HEAD_DIM))

    o_s = _sample_attn(page_table, blk, slopes_head, rows(q_s, BF16), rows(k_s, BF16),
                       rows(v_s, BF16), cache_k_t, cache_v_t, past_len)
    o_s = o_s[:, :, 0, :].reshape(nseq, d).astype(BF16)
    ys = _attn_out_mlp(o_s, xs, wo, g_mlp[0], w1[0], w2[0], nseq)
    prev = state_conv[0]
    y_sample, u_s = _sample_conv_mlp(ys, prev[:, 0], prev[:, 1], g_mix[1], wcin, cw, wcout,
                                     g_mlp[1], w1[1], w2[1], g_fin)

    k_prompt = jnp.swapaxes(k_t, -1, -2)[None]
    v_prompt = jnp.swapaxes(v_t, -1, -2)[None]
    k_sample = k_s.reshape(1, nseq, N_HEADS, 1, HEAD_DIM)
    v_sample = v_s.reshape(1, nseq, N_HEADS, 1, HEAD_DIM)
    conv_sample = jnp.stack([prev[:, 1], u_s], axis=1)[None]
    return (y_prompt, y_sample.reshape(nseq, 1, d), k_prompt, v_prompt, k_sample, v_sample,
            conv_prompt[None], conv_sample)
```

```python
import functools

import jax
import jax.numpy as jnp
from jax import lax
from jax.experimental import pallas as pl
from jax.experimental.pallas import tpu as pltpu

D_MODEL = 1024
N_HEADS = 16
HEAD_DIM = D_MODEL // N_HEADS
BLOCK = 256
TOPK = 3
PAGE_SIZE = 128
PAGES_PER_BLOCK = BLOCK // PAGE_SIZE
CONV_WIDTH = 3
D_FF = 4 * D_MODEL
NORM_EPS = 1e-5
QK_SCALE = HEAD_DIM ** -0.5
LOG2_E = 1.4426950408889634

F32 = jnp.float32
BF16 = jnp.bfloat16

HEADS_PER_STEP = 4
ROUTE_ROWS = 24
QK_WIDTH = 128
BIAS_TERMS = 3
V_ROWS = 80
TOKEN_TILE = 256
SAMPLE_ROWS = 8
SAMPLE_HEADS_PER_STEP = 4
BLOCKS_PER_SUM_STEP = 8
VMEM_LIMIT_BYTES = 56 * 1024 * 1024

_NT = (((1,), (1,)), ((), ()))


def _dot(a, b):
    return jnp.dot(a, b, preferred_element_type=F32)


def _dot_nt(a, b):
    return lax.dot_general(a, b, _NT, preferred_element_type=F32)


def _rms_norm(x, g):
    return (x * lax.rsqrt(jnp.mean(x * x, axis=-1, keepdims=True) + NORM_EPS)) * g


def _sqrelu_mlp(y, g, w1_ref, w2_ref):
    hn = _rms_norm(y, g).astype(BF16)
    hid = jnp.maximum(_dot(hn, w1_ref[...]), 0.0)
    return _dot((hid * hid).astype(BF16), w2_ref[...])


def _params(*semantics):
    return pltpu.CompilerParams(dimension_semantics=semantics,
                                vmem_limit_bytes=VMEM_LIMIT_BYTES)


def _resident(shape):
    zeros = (0,) * len(shape)
    return pl.BlockSpec(shape, lambda *_: zeros, pipeline_mode=pl.Buffered(1))


def _prompt_qkv_kernel(x_ref, g_ref, wq_t_ref, wk_ref, wk_t_ref, wv_t_ref, kbias_ref,
                       q_t_ref, k_ref, k_t_ref, v_t_ref, v_t_bf_ref, ksum_ref):
    hn = _rms_norm(x_ref[0], g_ref[...]).astype(BF16)
    t = hn.shape[0]
    q_t = _dot_nt(wq_t_ref[...], hn)
    q_pad = QK_WIDTH - HEAD_DIM
    ones_rows = jnp.where(lax.broadcasted_iota(jnp.int32, (q_pad, t), 0) < BIAS_TERMS,
                          1.0, 0.0).astype(BF16)
    for h in range(N_HEADS):
        q_t_ref[0, h * QK_WIDTH:h * QK_WIDTH + HEAD_DIM, :] = (
            q_t[h * HEAD_DIM:(h + 1) * HEAD_DIM].astype(BF16))
        q_t_ref[0, h * QK_WIDTH + HEAD_DIM:(h + 1) * QK_WIDTH, :] = ones_rows

    k = _dot(hn, wk_ref[...])
    for j in range(t // BLOCK):
        k_j = k[j * BLOCK:(j + 1) * BLOCK]
        ksum_ref[0, j] = jnp.sum(k_j, axis=0, keepdims=True)
        k_ref[0, j * BLOCK:(j + 1) * BLOCK, :] = (k_j + kbias_ref[...]).astype(BF16)
    k_t_ref[0] = _dot_nt(wk_t_ref[...], hn).reshape(N_HEADS, HEAD_DIM, t)

    v_t = _dot_nt(wv_t_ref[...], hn)
    v_t_ref[0] = v_t.reshape(N_HEADS, HEAD_DIM, t)
    v_pad = V_ROWS - HEAD_DIM
    ones_row = jnp.where(lax.broadcasted_iota(jnp.int32, (v_pad, BLOCK), 0) == 0,
                         1.0, 0.0).astype(BF16)
    for j in range(t // BLOCK):
        for h in range(N_HEADS):
            v_t_bf_ref[0, j, h * V_ROWS:h * V_ROWS + HEAD_DIM, :] = (
                v_t[h * HEAD_DIM:(h + 1) * HEAD_DIM, j * BLOCK:(j + 1) * BLOCK].astype(BF16))
            v_t_bf_ref[0, j, h * V_ROWS + HEAD_DIM:(h + 1) * V_ROWS, :] = ones_row


def _prompt_qkv(x, g, wq_t, wk_wide, wk_t, wv_t, kbias):
    b, s, d = x.shape
    t = TOKEN_TILE
    nb = s // BLOCK
    nbt = t // BLOCK
    dq = N_HEADS * QK_WIDTH
    dv = N_HEADS * V_ROWS
    return pl.pallas_call(
        _prompt_qkv_kernel,
        grid=(b, s // t),
        in_specs=[
            pl.BlockSpec((1, t, d), lambda i, j: (i, j, 0)),
            _resident((1, d)),
            _resident((d, d)), _resident((d, dq)), _resident((d, d)), _resident((d, d)),
            _resident((BLOCK, dq)),
        ],
        out_specs=[
            pl.BlockSpec((1, dq, t), lambda i, j: (i, 0, j)),
            pl.BlockSpec((1, t, dq), lambda i, j: (i, j, 0)),
            pl.BlockSpec((1, N_HEADS, HEAD_DIM, t), lambda i, j: (i, 0, 0, j)),
            pl.BlockSpec((1, N_HEADS, HEAD_DIM, t), lambda i, j: (i, 0, 0, j)),
            pl.BlockSpec((1, nbt, dv, BLOCK), lambda i, j: (i, j, 0, 0)),
            pl.BlockSpec((1, nbt, 1, dq), lambda i, j: (i, j, 0, 0)),
        ],
        out_shape=[
            jax.ShapeDtypeStruct((b, dq, s), BF16),
            jax.ShapeDtypeStruct((b, s, dq), BF16),
            jax.ShapeDtypeStruct((b, N_HEADS, HEAD_DIM, s), F32),
            jax.ShapeDtypeStruct((b, N_HEADS, HEAD_DIM, s), F32),
            jax.ShapeDtypeStruct((b, nb, dv, BLOCK), BF16),
            jax.ShapeDtypeStruct((b, nb, 1, dq), F32),
        ],
        compiler_params=_params("parallel", "parallel"),
        name="prompt_qkv",
    )(x, g, wq_t, wk_wide, wk_t, wv_t, kbias)


def _select_routed(gate, c):
    pidx = lax.broadcasted_iota(jnp.int32, gate.shape, 0)
    valid = (pidx >= 1) & (pidx <= c)
    g = jnp.where(valid, gate, -jnp.inf)
    sel = jnp.where(pidx == 0, 1.0, 0.0)
    for _ in range(TOPK):
        mx = jnp.max(g, axis=0, keepdims=True)
        first = jnp.min(jnp.where(g == mx, pidx, gate.shape[0]), axis=0, keepdims=True)
        pick = (pidx == first) & valid
        sel = jnp.where(pick, 1.0, sel)
        g = jnp.where(pick, -jnp.inf, g)
    return sel


def _prompt_attn_kernel(slope_ref, q_t_ref, k_ref, v_t_ref, ksum_ref, o_ref, sel_ref, *scratch):
    heads = range(HEADS_PER_STEP)
    scratch = iter(scratch)
    m_ref = [next(scratch) for _ in heads]
    acc_ref = [next(scratch) for _ in heads]
    alpha_ref = [[next(scratch) for _ in heads] for _ in range(2)]
    z_ref = [[next(scratch) for _ in heads] for _ in range(2)]
    p_ref = [[next(scratch) for _ in heads] for _ in range(2)]
    c = pl.program_id(2)
    nb = v_t_ref.shape[1]
    q = [q_t_ref[0, hh * QK_WIDTH:(hh + 1) * QK_WIDTH, :] for hh in heads]
    slope = [slope_ref[0, hh:hh + 1, :] for hh in heads]

    def block_of(pos):
        if isinstance(pos, int) and pos == 0:
            return c
        return jnp.where(pos == 0, c, jnp.clip(pos - 1, 0, nb - 1))

    def scores(pos, slot, hs=heads, bias=None):
        n = block_of(pos)
        rows = pl.ds(pl.multiple_of(n * BLOCK, BLOCK), BLOCK)
        for hh in hs:
            z = _dot(k_ref[0, rows, hh * QK_WIDTH:(hh + 1) * QK_WIDTH], q[hh])
            z_ref[slot][hh][...] = z if bias is None else z + bias

    def softmax(pos, slot, hs=heads):
        n = block_of(pos)
        for hh in hs:
            z = z_ref[slot][hh][...]
            off = slope[hh] * jnp.full(slope[hh].shape, (c - n) * BLOCK, jnp.int32).astype(F32)
            routed = sel_ref[hh, pl.ds(pos, 1), :] > 0.5
            m_old = m_ref[hh][...]
            m_new = jnp.where(routed, jnp.maximum(m_old, jnp.max(z, axis=0, keepdims=True) - off), m_old)
            m_ref[hh][...] = m_new
            alpha_ref[slot][hh][...] = jnp.exp2(m_old - m_new)
            p_ref[slot][hh][...] = jnp.exp2(z - jnp.where(routed, m_new + off, jnp.inf)).astype(BF16)

    def accumulate(pos, slot, hs=heads):
        n = block_of(pos)
        for hh in hs:
            pv = _dot(v_t_ref[0, n, hh * V_ROWS:(hh + 1) * V_ROWS, :], p_ref[slot][hh][...])
            acc_ref[hh][...] = alpha_ref[slot][hh][...] * acc_ref[hh][...] + pv

    kmean = (ksum_ref[0] * (1.0 / BLOCK)).astype(BF16)
    for hh in heads:
        sel_ref[hh] = _select_routed(_dot(kmean[:, hh * QK_WIDTH:(hh + 1) * QK_WIDTH], q[hh]), c)

    for hh in heads:
        m_ref[hh][...] = jnp.full(m_ref[hh].shape, -1e30, F32)
        acc_ref[hh][...] = jnp.zeros(acc_ref[hh].shape, F32)
        alpha_ref[1][hh][...] = jnp.ones(alpha_ref[1][hh].shape, F32)
        p_ref[1][hh][...] = jnp.zeros(p_ref[1][hh].shape, BF16)
    ki = lax.broadcasted_iota(jnp.int32, (BLOCK, BLOCK), 0)
    qj = lax.broadcasted_iota(jnp.int32, (BLOCK, BLOCK), 1)
    scores(0, 0, bias=jnp.where(ki <= qj, 0.0, -jnp.inf))

    def two_positions(s, carry):
        pos = 2 * s
        accumulate(pos - 1, 1)
        scores(pos + 1, 1)
        softmax(pos, 0)
        accumulate(pos, 0)
        scores(pos + 2, 0)
        softmax(pos + 1, 1)
        return carry

    n_pairs = lax.shift_right_logical(c + 2, 1)
    lax.fori_loop(0, n_pairs, two_positions, 0)
    accumulate(2 * n_pairs - 1, 1)

    outs = []
    for hh in heads:
        acc = acc_ref[hh][...]
        outs.append(acc[:HEAD_DIM] / acc[HEAD_DIM:HEAD_DIM + 1])
    o_ref[0] = jnp.concatenate(outs, axis=0).T.astype(BF16)


def _prompt_attn(slopes, q_t, k, v_t_blk, ksum_pos):
    b, s, dq = k.shape
    nb = s // BLOCK
    n_pos = ksum_pos.shape[1]
    n_steps = N_HEADS // HEADS_PER_STEP
    wq = HEADS_PER_STEP * QK_WIDTH
    wv = HEADS_PER_STEP * V_ROWS
    wo = HEADS_PER_STEP * HEAD_DIM
    assert n_pos >= nb + 2, "softmax reads the routing row of one position past the last"
    tile = (HEADS_PER_STEP, BLOCK, BLOCK)
    return pl.pallas_call(
        _prompt_attn_kernel,
        grid=(b, n_steps, nb),
        in_specs=[
            pl.BlockSpec((1, HEADS_PER_STEP, BLOCK), lambda i, h, c: (h, 0, 0)),
            pl.BlockSpec((1, wq, BLOCK), lambda i, h, c: (i, h, c)),
            pl.BlockSpec((1, s, wq), lambda i, h, c: (i, 0, h)),
            pl.BlockSpec((1, nb, wv, BLOCK), lambda i, h, c: (i, 0, h, 0)),
            pl.BlockSpec((1, n_pos, wq), lambda i, h, c: (i, 0, h)),
        ],
        out_specs=pl.BlockSpec((1, BLOCK, wo), lambda i, h, c: (i, c, h)),
        out_shape=jax.ShapeDtypeStruct((b, s, N_HEADS * HEAD_DIM), BF16),
        scratch_shapes=(
            [pltpu.VMEM((HEADS_PER_STEP, n_pos, BLOCK), F32)]
            + [pltpu.VMEM((1, BLOCK), F32)] * HEADS_PER_STEP
            + [pltpu.VMEM((V_ROWS, BLOCK), F32)] * HEADS_PER_STEP
            + [pltpu.VMEM((1, BLOCK), F32)] * (2 * HEADS_PER_STEP)
            + [pltpu.VMEM((BLOCK, BLOCK), F32)] * (2 * HEADS_PER_STEP)
            + [pltpu.VMEM((BLOCK, BLOCK), BF16)] * (2 * HEADS_PER_STEP)
        ),
        compiler_params=_params("parallel", "parallel", "arbitrary"),
        name="prompt_attn",
    )(slopes, q_t, k, v_t_blk, ksum_pos)


def _attn_out_mlp_kernel(o_ref, x_ref, wo_ref, g_ref, w1_ref, w2_ref, y_ref):
    y = x_ref[...] + _dot(o_ref[...], wo_ref[...])
    y_ref[...] = y + _sqrelu_mlp(y, g_ref[...], w1_ref, w2_ref)


def _attn_out_mlp(o, x, wo, g, w1, w2, tile):
    m, d = x.shape
    return pl.pallas_call(
        _attn_out_mlp_kernel,
        grid=(m // tile,),
        in_specs=[
            pl.BlockSpec((tile, d), lambda i: (i, 0)),
            pl.BlockSpec((tile, d), lambda i: (i, 0)),
            _resident((d, d)), _resident((1, d)),
            _resident((d, D_FF)), _resident((D_FF, d)),
        ],
        out_specs=pl.BlockSpec((tile, d), lambda i: (i, 0)),
        out_shape=jax.ShapeDtypeStruct((m, d), F32),
        compiler_params=_params("parallel"),
        name="attn_out_mlp",
    )(o, x, wo, g, w1, w2)


def _conv_layer_tail(y, bg, conv, wcout_ref, g_mlp, w1_ref, w2_ref, g_final):
    y = y + _dot((bg * conv).astype(BF16), wcout_ref[...])
    y = y + _sqrelu_mlp(y, g_mlp, w1_ref, w2_ref)
    return _rms_norm(y, g_final)


def _gates(y, g_mix, wcin_ref):
    d = y.shape[-1]
    z = _dot(_rms_norm(y, g_mix).astype(BF16), wcin_ref[...])
    return z[:, :d], z[:, d:2 * d] * z[:, 2 * d:]


def _prompt_conv_mlp_kernel(y_ref, g_mix_ref, wcin_ref, cw_ref, wcout_ref, g_mlp_ref,
                            w1_ref, w2_ref, g_fin_ref, out_ref, state_ref, carry_ref):
    j = pl.program_id(1)

    @pl.when(j == 0)
    def _():
        carry_ref[...] = jnp.zeros_like(carry_ref)

    y = y_ref[0]
    t = y.shape[0]
    bg, u = _gates(y, g_mix_ref[...], wcin_ref)
    r = lax.broadcasted_iota(jnp.int32, u.shape, 0)
    prev2 = carry_ref[0:1, :]
    prev1 = carry_ref[1:2, :]
    u1 = jnp.where(r == 0, prev1, pltpu.roll(u, 1, 0))
    u2 = jnp.where(r == 0, prev2, jnp.where(r == 1, prev1, pltpu.roll(u, 2, 0)))
    conv = cw_ref[0:1, :] * u2 + cw_ref[1:2, :] * u1 + cw_ref[2:3, :] * u
    carry_ref[...] = u[t - (CONV_WIDTH - 1):, :]
    out_ref[0] = _conv_layer_tail(y, bg, conv, wcout_ref, g_mlp_ref[...],
                                  w1_ref, w2_ref, g_fin_ref[...])

    @pl.when(j == pl.num_programs(1) - 1)
    def _():
        state_ref[0] = u[t - (CONV_WIDTH - 1):, :]


def _prompt_conv_mlp(y, g_mix, wcin, cw, wcout, g_mlp, w1, w2, g_fin):
    b, s, d = y.shape
    t = TOKEN_TILE
    return pl.pallas_call(
        _prompt_conv_mlp_kernel,
        grid=(b, s // t),
        in_specs=[
            pl.BlockSpec((1, t, d), lambda i, j: (i, j, 0)),
            _resident((1, d)), _resident((d, 3 * d)), _resident((CONV_WIDTH, d)),
            _resident((d, d)), _resident((1, d)),
            _resident((d, D_FF)), _resident((D_FF, d)), _resident((1, d)),
        ],
        out_specs=[
            pl.BlockSpec((1, t, d), lambda i, j: (i, j, 0)),
            pl.BlockSpec((1, CONV_WIDTH - 1, d), lambda i, j: (i, 0, 0)),
        ],
        out_shape=[
            jax.ShapeDtypeStruct((b, s, d), F32),
            jax.ShapeDtypeStruct((b, CONV_WIDTH - 1, d), F32),
        ],
        scratch_shapes=[pltpu.VMEM((CONV_WIDTH - 1, d), F32)],
        compiler_params=_params("parallel", "arbitrary"),
        name="prompt_conv_mlp",
    )(y, g_mix, wcin, cw, wcout, g_mlp, w1, w2, g_fin)


def _sample_conv_mlp_kernel(y_ref, prev2_ref, prev1_ref, g_mix_ref, wcin_ref, cw_ref,
                            wcout_ref, g_mlp_ref, w1_ref, w2_ref, g_fin_ref, out_ref, u_ref):
    y = y_ref[...]
    bg, u = _gates(y, g_mix_ref[...], wcin_ref)
    conv = cw_ref[0:1, :] * prev2_ref[...] + cw_ref[1:2, :] * prev1_ref[...] + cw_ref[2:3, :] * u
    u_ref[...] = u
    out_ref[...] = _conv_layer_tail(y, bg, conv, wcout_ref, g_mlp_ref[...],
                                    w1_ref, w2_ref, g_fin_ref[...])


def _sample_conv_mlp(y, prev2, prev1, g_mix, wcin, cw, wcout, g_mlp, w1, w2, g_fin):
    m, d = y.shape
    return pl.pallas_call(
        _sample_conv_mlp_kernel,
        grid=(1,),
        in_specs=[
            _resident((m, d)), _resident((m, d)), _resident((m, d)),
            _resident((1, d)), _resident((d, 3 * d)), _resident((CONV_WIDTH, d)),
            _resident((d, d)), _resident((1, d)),
            _resident((d, D_FF)), _resident((D_FF, d)), _resident((1, d)),
        ],
        out_specs=[pl.BlockSpec((m, d), lambda i: (0, 0)), pl.BlockSpec((m, d), lambda i: (0, 0))],
        out_shape=[jax.ShapeDtypeStruct((m, d), F32), jax.ShapeDtypeStruct((m, d), F32)],
        compiler_params=_params("arbitrary"),
        name="sample_conv_mlp",
    )(y, prev2, prev1, g_mix, wcin, cw, wcout, g_mlp, w1, w2, g_fin)


def _sample_qkv_kernel(x_ref, g_ref, w_ref, qkv_ref):
    qkv_ref[...] = _dot(_rms_norm(x_ref[...], g_ref[...]).astype(BF16), w_ref[...])


def _sample_qkv(x, g, wqkv):
    m, d = x.shape
    return pl.pallas_call(
        _sample_qkv_kernel,
        grid=(1,),
        in_specs=[_resident((m, d)), _resident((1, d)), _resident((d, 3 * d))],
        out_specs=pl.BlockSpec((m, 3 * d), lambda i: (0, 0)),
        out_shape=jax.ShapeDtypeStruct((m, 3 * d), F32),
        compiler_params=_params("arbitrary"),
        name="sample_qkv",
    )(x, g, wqkv)


def _block_key_sum(page_refs):
    nh, hd, ps = page_refs[0].shape[-3:]
    d = nh * hd
    blk = page_refs[0][0, 0].reshape(d, ps)
    for ref in page_refs[1:]:
        blk = blk + ref[0, 0].reshape(d, ps)
    pieces = [jnp.sum(blk[r:r + ps, :].T, axis=0, keepdims=True) for r in range(0, d, ps)]
    return jnp.concatenate(pieces, axis=1)


def _cache_block_sum_kernel(pt_ref, *refs):
    page_refs, out_ref = refs[:-1], refs[-1]
    for j in range(BLOCKS_PER_SUM_STEP):
        out_ref[0, j:j + 1, :] = _block_key_sum(
            page_refs[PAGES_PER_BLOCK * j:PAGES_PER_BLOCK * (j + 1)])


def _cache_block_sums(page_table, cache_t):
    nseq, n_pages = page_table.shape
    _, _, nh, hd, ps = cache_t.shape
    pages_per_step = BLOCKS_PER_SUM_STEP * PAGES_PER_BLOCK
    n_blocks = n_pages // PAGES_PER_BLOCK

    def page_spec(i):
        return pl.BlockSpec(
            (1, 1, nh, hd, ps),
            lambda b, j, pt: (0, pt[b, j * pages_per_step + i], 0, 0, 0))

    return pl.pallas_call(
        _cache_block_sum_kernel,
        grid_spec=pltpu.PrefetchScalarGridSpec(
            num_scalar_prefetch=1,
            grid=(nseq, n_pages // pages_per_step),
            in_specs=[page_spec(i) for i in range(pages_per_step)],
            out_specs=pl.BlockSpec((1, BLOCKS_PER_SUM_STEP, nh * hd), lambda b, j, pt: (b, j, 0)),
        ),
        out_shape=jax.ShapeDtypeStruct((nseq, n_blocks, nh * hd), F32),
        compiler_params=_params("parallel", "arbitrary"),
        name="cache_block_sums",
    )(page_table, *([cache_t] * pages_per_step))


def _sample_gate_kernel(q_ref, csum_ref, blk_ref):
    q = q_ref[0]
    d = q.shape[-1]
    head_row = lax.broadcasted_iota(jnp.int32, (N_HEADS, d), 0)
    head_col = lax.broadcasted_iota(jnp.int32, (N_HEADS, d), 1) // HEAD_DIM
    q_heads = jnp.where(head_row == head_col, q, 0.0).astype(BF16)
    kmean = (csum_ref[0] * (1.0 / BLOCK)).astype(BF16)
    g = _dot_nt(q_heads, kmean)
    nb = g.shape[-1]
    lane = lax.broadcasted_iota(jnp.int32, g.shape, 1)
    out_lane = lax.broadcasted_iota(jnp.int32, blk_ref.shape[1:], 1)
    out = jnp.zeros(blk_ref.shape[1:], jnp.int32)
    for r in range(TOPK):
        mx = jnp.max(g, axis=1, keepdims=True)
        first = jnp.min(jnp.where(g == mx, lane, nb), axis=1, keepdims=True)
        out = jnp.where(out_lane == r, first, out)
        g = jnp.where(lane == first, -jnp.inf, g)
    blk_ref[0] = out


def _sample_gate(q, csum):
    nseq, nb, d = csum.shape
    return pl.pallas_call(
        _sample_gate_kernel,
        grid=(nseq,),
        in_specs=[
            pl.BlockSpec((1, 1, d), lambda b: (b, 0, 0)),
            pl.BlockSpec((1, nb, d), lambda b: (b, 0, 0)),
        ],
        out_specs=pl.BlockSpec((1, N_HEADS, PAGE_SIZE), lambda b: (b, 0, 0)),
        out_shape=jax.ShapeDtypeStruct((nseq, N_HEADS, PAGE_SIZE), jnp.int32),
        compiler_params=_params("parallel"),
        name="sample_gate",
    )(q, csum)


def _sample_attn_kernel(past_len, pt_ref, blk_ref, slope_ref, q_ref, knew_ref, vnew_ref, *refs):
    pages_per_head = TOPK * PAGES_PER_BLOCK
    n_pages = SAMPLE_HEADS_PER_STEP * pages_per_head
    k_refs, v_refs, o_ref = refs[:n_pages], refs[n_pages:2 * n_pages], refs[-1]
    b = pl.program_id(0)
    lane = lax.broadcasted_iota(jnp.int32, (SAMPLE_ROWS, PAGE_SIZE), 1)

    for hh in range(SAMPLE_HEADS_PER_STEP):
        h = pl.program_id(1) * SAMPLE_HEADS_PER_STEP + hh
        q = q_ref[0, hh]
        slope = slope_ref[hh]
        head_k = k_refs[hh * pages_per_head:(hh + 1) * pages_per_head]
        head_v = v_refs[hh * pages_per_head:(hh + 1) * pages_per_head]

        logits = []
        for s in range(TOPK):
            blk = blk_ref[b, h * TOPK + s]
            for i in range(PAGES_PER_BLOCK):
                k_t = head_k[s * PAGES_PER_BLOCK + i][0, 0, 0].astype(BF16)
                pos = blk * BLOCK + i * PAGE_SIZE + lane
                logits.append(_dot(q, k_t) - slope * (past_len - pos).astype(F32))
        k_new = knew_ref[0, hh].astype(F32)
        self_logit = jnp.sum(q.astype(F32) * k_new, axis=-1, keepdims=True)

        m = self_logit
        for z in logits:
            m = jnp.maximum(m, jnp.max(z, axis=-1, keepdims=True))
        p_self = jnp.exp(self_logit - m)
        l = p_self
        acc = p_self.astype(BF16).astype(F32) * vnew_ref[0, hh].astype(F32)
        for z, v_ref in zip(logits, head_v):
            p = jnp.exp(z - m)
            l = l + jnp.sum(p, axis=-1, keepdims=True)
            acc = acc + _dot_nt(p.astype(BF16), v_ref[0, 0, 0].astype(BF16))
        o_ref[0, hh] = acc / l


def _sample_attn(page_table, blk, slopes, q, k_new, v_new, cache_k_t, cache_v_t, past_len):
    nseq = page_table.shape[0]
    _, _, nh, hd, ps = cache_k_t.shape
    hps = SAMPLE_HEADS_PER_STEP

    def page_spec(hh, s, i):
        def index(b, g, pt, bk):
            h = g * hps + hh
            return (0, pt[b, bk[b, h * TOPK + s] * PAGES_PER_BLOCK + i], h, 0, 0)
        return pl.BlockSpec((1, 1, 1, hd, ps), index)

    page_specs = [page_spec(hh, s, i) for hh in range(hps)
                  for s in range(TOPK) for i in range(PAGES_PER_BLOCK)]
    row_spec = pl.BlockSpec((1, hps, SAMPLE_ROWS, hd), lambda b, g, pt, bk: (b, g, 0, 0))
    n_pages = len(page_specs)
    return pl.pallas_call(
        functools.partial(_sample_attn_kernel, past_len),
        grid_spec=pltpu.PrefetchScalarGridSpec(
            num_scalar_prefetch=2,
            grid=(nseq, nh // hps),
            in_specs=[pl.BlockSpec((hps, 1, ps), lambda b, g, pt, bk: (g, 0, 0)),
                      row_spec, row_spec, row_spec] + page_specs + page_specs,
            out_specs=row_spec,
        ),
        out_shape=jax.ShapeDtypeStruct((nseq, nh, SAMPLE_ROWS, hd), F32),
        compiler_params=_params("parallel", "arbitrary"),
        name="sample_attn",
    )(page_table, blk, slopes, q, k_new, v_new,
      *([cache_k_t] * n_pages), *([cache_v_t] * n_pages))


def _widen_heads(w, width):
    d = w.shape[0]
    w = w.reshape(d, N_HEADS, HEAD_DIM)
    w = jnp.pad(w, ((0, 0), (0, 0), (0, width - HEAD_DIM)))
    return w.reshape(d, N_HEADS * width)


def _alibi_key_bias(slopes):
    rest = slopes[None, :] * jnp.arange(BLOCK, dtype=F32)[:, None]
    terms = []
    for _ in range(BIAS_TERMS):
        bits = lax.bitcast_convert_type(rest, jnp.uint32) & jnp.uint32(0xFFFF0000)
        term = lax.bitcast_convert_type(bits, F32)
        terms.append(term)
        rest = rest - term
    bias = jnp.stack(terms, axis=-1)
    bias = jnp.pad(bias, ((0, 0), (0, 0), (HEAD_DIM, QK_WIDTH - HEAD_DIM - BIAS_TERMS)))
    return bias.reshape(BLOCK, N_HEADS * QK_WIDTH)


def kernel(x_prompt, x_sample, cache_k, cache_v, state_conv, page_table, norm_mix, norm_mlp,
           norm_final, w_qkv, w_attn_out, w_conv_in, conv_w, w_conv_out, w_mlp_in, w_mlp_out):
    b, s, d = x_prompt.shape
    nseq, dec_seq, _ = x_sample.shape
    n_pages = page_table.shape[1]
    past_len = n_pages * PAGE_SIZE
    assert d == D_MODEL and s % TOKEN_TILE == 0 and TOKEN_TILE % BLOCK == 0
    assert dec_seq == 1 and past_len % BLOCK == 0, "the new token must open a fresh key block"
    assert n_pages % (BLOCKS_PER_SUM_STEP * PAGES_PER_BLOCK) == 0
    assert w_qkv.shape[0] == 1 and w_conv_in.shape[0] == 1 and w_mlp_in.shape[0] == 2

    wqkv = w_qkv[0]
    wq = wqkv[:, :d] * QK_SCALE
    wq_t = (wq * LOG2_E).T.astype(BF16)
    wk = wqkv[:, d:2 * d].astype(BF16)
    wk_t = wk.T
    wv_t = wqkv[:, 2 * d:].T.astype(BF16)
    wo = w_attn_out[0].astype(BF16)
    wcin = w_conv_in[0].astype(BF16)
    wcout = w_conv_out[0].astype(BF16)
    w1 = w_mlp_in.astype(BF16)
    w2 = w_mlp_out.astype(BF16)
    g_mix = norm_mix.reshape(2, 1, d)
    g_mlp = norm_mlp.reshape(2, 1, d)
    g_fin = norm_final.reshape(1, d)
    cw = conv_w[0]

    slopes = jnp.exp2(-8.0 * jnp.arange(1, N_HEADS + 1, dtype=F32) / N_HEADS)
    slopes_log2 = slopes * LOG2_E
    slopes_step = jnp.broadcast_to(
        slopes_log2.reshape(N_HEADS // HEADS_PER_STEP, HEADS_PER_STEP, 1),
        (N_HEADS // HEADS_PER_STEP, HEADS_PER_STEP, BLOCK))
    slopes_head = jnp.broadcast_to(slopes.reshape(N_HEADS, 1, 1), (N_HEADS, 1, PAGE_SIZE))

    nb = s // BLOCK
    q_t, k_bf, k_t, v_t, v_t_blk, ksum = _prompt_qkv(
        x_prompt, g_mix[0], wq_t, _widen_heads(wk, QK_WIDTH), wk_t, wv_t,
        _alibi_key_bias(slopes_log2))
    ksum_pos = jnp.pad(ksum.reshape(b, nb, N_HEADS * QK_WIDTH),
                       ((0, 0), (1, ROUTE_ROWS - nb - 1), (0, 0)))
    attn = _prompt_attn(slopes_step, q_t, k_bf, v_t_blk, ksum_pos)
    y = _attn_out_mlp(attn.reshape(b * s, d), x_prompt.reshape(b * s, d),
                      wo, g_mlp[0], w1[0], w2[0], TOKEN_TILE)
    y_prompt, conv_prompt = _prompt_conv_mlp(y.reshape(b, s, d), g_mix[1], wcin, cw, wcout,
                                             g_mlp[1], w1[1], w2[1], g_fin)

    xs = x_sample.reshape(nseq, d)
    qkv = _sample_qkv(xs, g_mix[0], jnp.concatenate([wq.astype(BF16), wk, wv_t.T], axis=1))
    q_s, k_s, v_s = qkv[:, :d], qkv[:, d:2 * d], qkv[:, 2 * d:]
    cache_k_t = jnp.swapaxes(cache_k, -1, -2)
    cache_v_t = jnp.swapaxes(cache_v, -1, -2)
    csum = _cache_block_sums(page_table, cache_k_t)
    blk = _sample_gate(q_s.reshape(nseq, 1, d), csum)[:, :, :TOPK].reshape(nseq, N_HEADS * TOPK)

    def rows(a, dtype):
        a = a.reshape(nseq, N_HEADS, 1, HEAD_DIM).astype(dtype)
        return jnp.broadcast_to(a, (nseq, N_HEADS, SAMPLE_ROWS, HEAD_DIM))

    o_s = _sample_attn(page_table, blk, slopes_head, rows(q_s, BF16), rows(k_s, BF16),
                       rows(v_s, BF16), cache_k_t, cache_v_t, past_len)
    o_s = o_s[:, :, 0, :].reshape(nseq, d).astype(BF16)
    ys = _attn_out_mlp(o_s, xs, wo, g_mlp[0], w1[0], w2[0], nseq)
    prev = state_conv[0]
    y_sample, u_s = _sample_conv_mlp(ys, prev[:, 0], prev[:, 1], g_mix[1], wcin, cw, wcout,
                                     g_mlp[1], w1[1], w2[1], g_fin)

    k_prompt = jnp.swapaxes(k_t, -1, -2)[None]
    v_prompt = jnp.swapaxes(v_t, -1, -2)[None]
    k_sample = k_s.reshape(1, nseq, N_HEADS, 1, HEAD_DIM)
    v_sample = v_s.reshape(1, nseq, N_HEADS, 1, HEAD_DIM)
    conv_sample = jnp.stack([prev[:, 1], u_s], axis=1)[None]
    return (y_prompt, y_sample.reshape(nseq, 1, d), k_prompt, v_prompt, k_sample, v_sample,
            conv_prompt[None], conv_sample)
```

```python
import functools

import jax
import jax.numpy as jnp
from jax import lax
from jax.experimental import pallas as pl
from jax.experimental.pallas import tpu as pltpu

D_MODEL = 1024
N_HEADS = 16
HEAD_DIM = D_MODEL // N_HEADS
BLOCK = 256
TOPK = 3
PAGE_SIZE = 128
PAGES_PER_BLOCK = BLOCK // PAGE_SIZE
CONV_WIDTH = 3
D_FF = 4 * D_MODEL
NORM_EPS = 1e-5
QK_SCALE = HEAD_DIM ** -0.5
LOG2_E = 1.4426950408889634

F32 = jnp.float32
BF16 = jnp.bfloat16

HEADS_PER_STEP = 4
ROUTE_ROWS = 24
QK_WIDTH = 128
BIAS_TERMS = 3
V_ROWS = 80
TOKEN_TILE = 256
SAMPLE_ROWS = 8
SAMPLE_HEADS_PER_STEP = 4
BLOCKS_PER_SUM_STEP = 8
VMEM_LIMIT_BYTES = 56 * 1024 * 1024

_NT = (((1,), (1,)), ((), ()))


def _dot(a, b):
    return jnp.dot(a, b, preferred_element_type=F32)


def _dot_nt(a, b):
    return lax.dot_general(a, b, _NT, preferred_element_type=F32)


def _rms_norm(x, g):
    return (x * lax.rsqrt(jnp.mean(x * x, axis=-1, keepdims=True) + NORM_EPS)) * g


def _sqrelu_mlp(y, g, w1_ref, w2_ref):
    hn = _rms_norm(y, g).astype(BF16)
    hid = jnp.maximum(_dot(hn, w1_ref[...]), 0.0)
    return _dot((hid * hid).astype(BF16), w2_ref[...])


def _params(*semantics):
    return pltpu.CompilerParams(dimension_semantics=semantics,
                                vmem_limit_bytes=VMEM_LIMIT_BYTES)


def _resident(shape):
    zeros = (0,) * len(shape)
    return pl.BlockSpec(shape, lambda *_: zeros, pipeline_mode=pl.Buffered(1))


def _prompt_qkv_kernel(x_ref, g_ref, wq_t_ref, wk_ref, wk_t_ref, wv_t_ref, kbias_ref,
                       q_t_ref, k_ref, k_t_ref, v_t_ref, v_t_bf_ref, ksum_ref):
    hn = _rms_norm(x_ref[0], g_ref[...]).astype(BF16)
    t = hn.shape[0]
    q_t = _dot_nt(wq_t_ref[...], hn)
    q_pad = QK_WIDTH - HEAD_DIM
    ones_rows = jnp.where(lax.broadcasted_iota(jnp.int32, (q_pad, t), 0) < BIAS_TERMS,
                          1.0, 0.0).astype(BF16)
    for h in range(N_HEADS):
        q_t_ref[0, h * QK_WIDTH:h * QK_WIDTH + HEAD_DIM, :] = (
            q_t[h * HEAD_DIM:(h + 1) * HEAD_DIM].astype(BF16))
        q_t_ref[0, h * QK_WIDTH + HEAD_DIM:(h + 1) * QK_WIDTH, :] = ones_rows

    k = _dot(hn, wk_ref[...])
    for j in range(t // BLOCK):
        k_j = k[j * BLOCK:(j + 1) * BLOCK]
        ksum_ref[0, j] = jnp.sum(k_j, axis=0, keepdims=True)
        k_ref[0, j * BLOCK:(j + 1) * BLOCK, :] = (k_j + kbias_ref[...]).astype(BF16)
    k_t_ref[0] = _dot_nt(wk_t_ref[...], hn).reshape(N_HEADS, HEAD_DIM, t)

    v_t = _dot_nt(wv_t_ref[...], hn)
    v_t_ref[0] = v_t.reshape(N_HEADS, HEAD_DIM, t)
    v_pad = V_ROWS - HEAD_DIM
    ones_row = jnp.where(lax.broadcasted_iota(jnp.int32, (v_pad, BLOCK), 0) == 0,
                         1.0, 0.0).astype(BF16)
    for j in range(t // BLOCK):
        for h in range(N_HEADS):
            v_t_bf_ref[0, j, h * V_ROWS:h * V_ROWS + HEAD_DIM, :] = (
                v_t[h * HEAD_DIM:(h + 1) * HEAD_DIM, j * BLOCK:(j + 1) * BLOCK].astype(BF16))
            v_t_bf_ref[0, j, h * V_ROWS + HEAD_DIM:(h + 1) * V_ROWS, :] = ones_row


def _prompt_qkv(x, g, wq_t, wk_wide, wk_t, wv_t, kbias):
    b, s, d = x.shape
    t = TOKEN_TILE
    nb = s // BLOCK
    nbt = t // BLOCK
    dq = N_HEADS * QK_WIDTH
    dv = N_HEADS * V_ROWS
    return pl.pallas_call(
        _prompt_qkv_kernel,
        grid=(b, s // t),
        in_specs=[
            pl.BlockSpec((1, t, d), lambda i, j: (i, j, 0)),
            _resident((1, d)),
            _resident((d, d)), _resident((d, dq)), _resident((d, d)), _resident((d, d)),
            _resident((BLOCK, dq)),
        ],
        out_specs=[
            pl.BlockSpec((1, dq, t), lambda i, j: (i, 0, j)),
            pl.BlockSpec((1, t, dq), lambda i, j: (i, j, 0)),
            pl.BlockSpec((1, N_HEADS, HEAD_DIM, t), lambda i, j: (i, 0, 0, j)),
            pl.BlockSpec((1, N_HEADS, HEAD_DIM, t), lambda i, j: (i, 0, 0, j)),
            pl.BlockSpec((1, nbt, dv, BLOCK), lambda i, j: (i, j, 0, 0)),
            pl.BlockSpec((1, nbt, 1, dq), lambda i, j: (i, j, 0, 0)),
        ],
        out_shape=[
            jax.ShapeDtypeStruct((b, dq, s), BF16),
            jax.ShapeDtypeStruct((b, s, dq), BF16),
            jax.ShapeDtypeStruct((b, N_HEADS, HEAD_DIM, s), F32),
            jax.ShapeDtypeStruct((b, N_HEADS, HEAD_DIM, s), F32),
            jax.ShapeDtypeStruct((b, nb, dv, BLOCK), BF16),
            jax.ShapeDtypeStruct((b, nb, 1, dq), F32),
        ],
        compiler_params=_params("parallel", "parallel"),
        name="prompt_qkv",
    )(x, g, wq_t, wk_wide, wk_t, wv_t, kbias)


def _select_routed(gate, c):
    pidx = lax.broadcasted_iota(jnp.int32, gate.shape, 0)
    valid = (pidx >= 1) & (pidx <= c)
    g = jnp.where(valid, gate, -jnp.inf)
    sel = jnp.where(pidx == 0, 1.0, 0.0)
    for _ in range(TOPK):
        mx = jnp.max(g, axis=0, keepdims=True)
        first = jnp.min(jnp.where(g == mx, pidx, gate.shape[0]), axis=0, keepdims=True)
        pick = (pidx == first) & valid
        sel = jnp.where(pick, 1.0, sel)
        g = jnp.where(pick, -jnp.inf, g)
    return sel


def _prompt_attn_kernel(slope_ref, q_t_ref, k_ref, v_t_ref, ksum_ref, o_ref, sel_ref, *scratch):
    heads = range(HEADS_PER_STEP)
    scratch = iter(scratch)
    m_ref = [next(scratch) for _ in heads]
    acc_ref = [next(scratch) for _ in heads]
    alpha_ref = [[next(scratch) for _ in heads] for _ in range(2)]
    z_ref = [[next(scratch) for _ in heads] for _ in range(2)]
    p_ref = [[next(scratch) for _ in heads] for _ in range(2)]
    c = pl.program_id(2)
    nb = v_t_ref.shape[1]
    q = [q_t_ref[0, hh * QK_WIDTH:(hh + 1) * QK_WIDTH, :] for hh in heads]
    slope = [slope_ref[0, hh:hh + 1, :] for hh in heads]

    def block_of(pos):
        if isinstance(pos, int) and pos == 0:
            return c
        return jnp.where(pos == 0, c, jnp.clip(pos - 1, 0, nb - 1))

    def scores(pos, slot, hs=heads, bias=None):
        n = block_of(pos)
        rows = pl.ds(pl.multiple_of(n * BLOCK, BLOCK), BLOCK)
        for hh in hs:
            z = _dot(k_ref[0, rows, hh * QK_WIDTH:(hh + 1) * QK_WIDTH], q[hh])
            z_ref[slot][hh][...] = z if bias is None else z + bias

    def softmax(pos, slot, hs=heads):
        n = block_of(pos)
        for hh in hs:
            z = z_ref[slot][hh][...]
            off = slope[hh] * jnp.full(slope[hh].shape, (c - n) * BLOCK, jnp.int32).astype(F32)
            routed = sel_ref[hh, pl.ds(pos, 1), :] > 0.5
            m_old = m_ref[hh][...]
            m_new = jnp.where(routed, jnp.maximum(m_old, jnp.max(z, axis=0, keepdims=True) - off), m_old)
            m_ref[hh][...] = m_new
            alpha_ref[slot][hh][...] = jnp.exp2(m_old - m_new)
            p_ref[slot][hh][...] = jnp.exp2(z - jnp.where(routed, m_new + off, jnp.inf)).astype(BF16)

    def accumulate(pos, slot, hs=heads):
        n = block_of(pos)
        for hh in hs:
            pv = _dot(v_t_ref[0, n, hh * V_ROWS:(hh + 1) * V_ROWS, :], p_ref[slot][hh][...])
            acc_ref[hh][...] = alpha_ref[slot][hh][...] * acc_ref[hh][...] + pv

    kmean = (ksum_ref[0] * (1.0 / BLOCK)).astype(BF16)
    for hh in heads:
        sel_ref[hh] = _select_routed(_dot(kmean[:, hh * QK_WIDTH:(hh + 1) * QK_WIDTH], q[hh]), c)

    for hh in heads:
        m_ref[hh][...] = jnp.full(m_ref[hh].shape, -1e30, F32)
        acc_ref[hh][...] = jnp.zeros(acc_ref[hh].shape, F32)
        alpha_ref[1][hh][...] = jnp.ones(alpha_ref[1][hh].shape, F32)
        p_ref[1][hh][...] = jnp.zeros(p_ref[1][hh].shape, BF16)
    ki = lax.broadcasted_iota(jnp.int32, (BLOCK, BLOCK), 0)
    qj = lax.broadcasted_iota(jnp.int32, (BLOCK, BLOCK), 1)
    scores(0, 0, bias=jnp.where(ki <= qj, 0.0, -jnp.inf))

    def two_positions(s, carry):
        pos = 2 * s
        accumulate(pos - 1, 1)
        scores(pos + 1, 1)
        softmax(pos, 0)
        accumulate(pos, 0)
        scores(pos + 2, 0)
        softmax(pos + 1, 1)
        return carry

    n_pairs = lax.shift_right_logical(c + 2, 1)
    lax.fori_loop(0, n_pairs, two_positions, 0)
    accumulate(2 * n_pairs - 1, 1)

    outs = []
    for hh in heads:
        acc = acc_ref[hh][...]
        outs.append(acc[:HEAD_DIM] / acc[HEAD_DIM:HEAD_DIM + 1])
    o_ref[0] = jnp.concatenate(outs, axis=0).T.astype(BF16)


def _prompt_attn(slopes, q_t, k, v_t_blk, ksum_pos):
    b, s, dq = k.shape
    nb = s // BLOCK
    n_pos = ksum_pos.shape[1]
    n_steps = N_HEADS // HEADS_PER_STEP
    wq = HEADS_PER_STEP * QK_WIDTH
    wv = HEADS_PER_STEP * V_ROWS
    wo = HEADS_PER_STEP * HEAD_DIM
    assert n_pos >= nb + 2, "softmax reads the routing row of one position past the last"
    tile = (HEADS_PER_STEP, BLOCK, BLOCK)
    return pl.pallas_call(
        _prompt_attn_kernel,
        grid=(b, n_steps, nb),
        in_specs=[
            pl.BlockSpec((1, HEADS_PER_STEP, BLOCK), lambda i, h, c: (h, 0, 0)),
            pl.BlockSpec((1, wq, BLOCK), lambda i, h, c: (i, h, c)),
            pl.BlockSpec((1, s, wq), lambda i, h, c: (i, 0, h)),
            pl.BlockSpec((1, nb, wv, BLOCK), lambda i, h, c: (i, 0, h, 0)),
            pl.BlockSpec((1, n_pos, wq), lambda i, h, c: (i, 0, h)),
        ],
        out_specs=pl.BlockSpec((1, BLOCK, wo), lambda i, h, c: (i, c, h)),
        out_shape=jax.ShapeDtypeStruct((b, s, N_HEADS * HEAD_DIM), BF16),
        scratch_shapes=(
            [pltpu.VMEM((HEADS_PER_STEP, n_pos, BLOCK), F32)]
            + [pltpu.VMEM((1, BLOCK), F32)] * HEADS_PER_STEP
            + [pltpu.VMEM((V_ROWS, BLOCK), F32)] * HEADS_PER_STEP
            + [pltpu.VMEM((1, BLOCK), F32)] * (2 * HEADS_PER_STEP)
            + [pltpu.VMEM((BLOCK, BLOCK), F32)] * (2 * HEADS_PER_STEP)
            + [pltpu.VMEM((BLOCK, BLOCK), BF16)] * (2 * HEADS_PER_STEP)
        ),
        compiler_params=_params("parallel", "parallel", "arbitrary"),
        name="prompt_attn",
    )(slopes, q_t, k, v_t_blk, ksum_pos)


def _attn_out_mlp_kernel(o_ref, x_ref, wo_ref, g_ref, w1_ref, w2_ref, y_ref):
    y = x_ref[...] + _dot(o_ref[...], wo_ref[...])
    y_ref[...] = y + _sqrelu_mlp(y, g_ref[...], w1_ref, w2_ref)


def _attn_out_mlp(o, x, wo, g, w1, w2, tile):
    m, d = x.shape
    return pl.pallas_call(
        _attn_out_mlp_kernel,
        grid=(m // tile,),
        in_specs=[
            pl.BlockSpec((tile, d), lambda i: (i, 0)),
            pl.BlockSpec((tile, d), lambda i: (i, 0)),
            _resident((d, d)), _resident((1, d)),
            _resident((d, D_FF)), _resident((D_FF, d)),
        ],
        out_specs=pl.BlockSpec((tile, d), lambda i: (i, 0)),
        out_shape=jax.ShapeDtypeStruct((m, d), F32),
        compiler_params=_params("parallel"),
        name="attn_out_mlp",
    )(o, x, wo, g, w1, w2)


class _PageStream:
    def __init__(self, page_table, cache_t, n_steps, seq0, step_of):
        n_total, n_pages = page_table.shape
        _, _, nh, hd, ps = cache_t.shape
        self.pages_per_step = BLOCKS_PER_SUM_STEP * PAGES_PER_BLOCK
        steps_per_seq = n_pages // self.pages_per_step
        self.nseq = n_steps // steps_per_seq
        assert self.nseq * steps_per_seq == n_steps
        self.width = nh * hd
        self.blocks = steps_per_seq * BLOCKS_PER_SUM_STEP
        self.operands = [cache_t] * self.pages_per_step

        def page_spec(k):
            def index(*args):
                step, pt = step_of(*args[:-1]), args[-1]
                seq = jnp.minimum(seq0 + step // steps_per_seq, n_total - 1)
                return (0, pt[seq, (step % steps_per_seq) * self.pages_per_step + k], 0, 0, 0)
            return pl.BlockSpec((1, 1, nh, hd, ps), index)

        def out_index(*args):
            step = step_of(*args[:-1])
            return (step // steps_per_seq, step % steps_per_seq, 0, 0)

        self.in_specs = [page_spec(k) for k in range(self.pages_per_step)]
        self.out_spec = pl.BlockSpec((1, 1, BLOCKS_PER_SUM_STEP, self.width), out_index)
        self.out_shape = jax.ShapeDtypeStruct(
            (self.nseq, steps_per_seq, BLOCKS_PER_SUM_STEP, self.width), F32)

    def finish(self, csum):
        return csum.reshape(self.nseq, self.blocks, self.width)


def _store_block_key_sums(page_refs, csum_ref):
    for j in range(csum_ref.shape[2]):
        csum_ref[0, 0, j:j + 1, :] = _block_key_sum(
            page_refs[PAGES_PER_BLOCK * j:PAGES_PER_BLOCK * (j + 1)])


def _attn_out_mlp_sums_kernel(pt_ref, o_ref, x_ref, wo_ref, g_ref, w1_ref, w2_ref, *refs):
    page_refs, y_ref, csum_ref = refs[:-2], refs[-2], refs[-1]
    _store_block_key_sums(page_refs, csum_ref)
    y = x_ref[...] + _dot(o_ref[...], wo_ref[...])
    y_ref[...] = y + _sqrelu_mlp(y, g_ref[...], w1_ref, w2_ref)


def _attn_out_mlp_with_block_sums(o, x, wo, g, w1, w2, tile, page_table, cache_t, seq0):
    m, d = x.shape
    n_steps = m // tile
    stream = _PageStream(page_table, cache_t, n_steps, seq0, lambda i: i)
    y, csum = pl.pallas_call(
        _attn_out_mlp_sums_kernel,
        grid_spec=pltpu.PrefetchScalarGridSpec(
            num_scalar_prefetch=1,
            grid=(n_steps,),
            in_specs=[
                pl.BlockSpec((tile, d), lambda i, pt: (i, 0)),
                pl.BlockSpec((tile, d), lambda i, pt: (i, 0)),
                _resident((d, d)), _resident((1, d)),
                _resident((d, D_FF)), _resident((D_FF, d)),
            ] + stream.in_specs,
            out_specs=[pl.BlockSpec((tile, d), lambda i, pt: (i, 0)), stream.out_spec],
        ),
        out_shape=[jax.ShapeDtypeStruct((m, d), F32), stream.out_shape],
        compiler_params=_params("parallel"),
        name="attn_out_mlp_sums",
    )(page_table, o, x, wo, g, w1, w2, *stream.operands)
    return y, stream.finish(csum)


def _conv_layer_tail(y, bg, conv, wcout_ref, g_mlp, w1_ref, w2_ref, g_final):
    y = y + _dot((bg * conv).astype(BF16), wcout_ref[...])
    y = y + _sqrelu_mlp(y, g_mlp, w1_ref, w2_ref)
    return _rms_norm(y, g_final)


def _gates(y, g_mix, wcin_ref):
    d = y.shape[-1]
    z = _dot(_rms_norm(y, g_mix).astype(BF16), wcin_ref[...])
    return z[:, :d], z[:, d:2 * d] * z[:, 2 * d:]


def _prompt_conv_mlp_kernel(pt_ref, y_ref, g_mix_ref, wcin_ref, cw_ref, wcout_ref, g_mlp_ref,
                            w1_ref, w2_ref, g_fin_ref, *refs):
    page_refs, (out_ref, state_ref, csum_ref, carry_ref) = refs[:-4], refs[-4:]
    j = pl.program_id(1)

    @pl.when(j == 0)
    def _():
        carry_ref[...] = jnp.zeros_like(carry_ref)

    _store_block_key_sums(page_refs, csum_ref)
    y = y_ref[0]
    t = y.shape[0]
    bg, u = _gates(y, g_mix_ref[...], wcin_ref)
    r = lax.broadcasted_iota(jnp.int32, u.shape, 0)
    prev2 = carry_ref[0:1, :]
    prev1 = carry_ref[1:2, :]
    u1 = jnp.where(r == 0, prev1, pltpu.roll(u, 1, 0))
    u2 = jnp.where(r == 0, prev2, jnp.where(r == 1, prev1, pltpu.roll(u, 2, 0)))
    conv = cw_ref[0:1, :] * u2 + cw_ref[1:2, :] * u1 + cw_ref[2:3, :] * u
    carry_ref[...] = u[t - (CONV_WIDTH - 1):, :]
    out_ref[0] = _conv_layer_tail(y, bg, conv, wcout_ref, g_mlp_ref[...],
                                  w1_ref, w2_ref, g_fin_ref[...])

    @pl.when(j == pl.num_programs(1) - 1)
    def _():
        state_ref[0] = u[t - (CONV_WIDTH - 1):, :]


def _prompt_conv_mlp(y, g_mix, wcin, cw, wcout, g_mlp, w1, w2, g_fin, page_table, cache_t, seq0):
    b, s, d = y.shape
    t = TOKEN_TILE
    steps = s // t
    stream = _PageStream(page_table, cache_t, b * steps, seq0, lambda i, j: i * steps + j)
    out, state, csum = pl.pallas_call(
        _prompt_conv_mlp_kernel,
        grid_spec=pltpu.PrefetchScalarGridSpec(
            num_scalar_prefetch=1,
            grid=(b, steps),
            in_specs=[
                pl.BlockSpec((1, t, d), lambda i, j, pt: (i, j, 0)),
                _resident((1, d)), _resident((d, 3 * d)), _resident((CONV_WIDTH, d)),
                _resident((d, d)), _resident((1, d)),
                _resident((d, D_FF)), _resident((D_FF, d)), _resident((1, d)),
            ] + stream.in_specs,
            out_specs=[
                pl.BlockSpec((1, t, d), lambda i, j, pt: (i, j, 0)),
                pl.BlockSpec((1, CONV_WIDTH - 1, d), lambda i, j, pt: (i, 0, 0)),
                stream.out_spec,
            ],
            scratch_shapes=[pltpu.VMEM((CONV_WIDTH - 1, d), F32)],
        ),
        out_shape=[
            jax.ShapeDtypeStruct((b, s, d), F32),
            jax.ShapeDtypeStruct((b, CONV_WIDTH - 1, d), F32),
            stream.out_shape,
        ],
        compiler_params=_params("parallel", "arbitrary"),
        name="prompt_conv_mlp",
    )(page_table, y, g_mix, wcin, cw, wcout, g_mlp, w1, w2, g_fin, *stream.operands)
    return out, state, stream.finish(csum)


def _sample_conv_mlp_kernel(y_ref, prev2_ref, prev1_ref, g_mix_ref, wcin_ref, cw_ref,
                            wcout_ref, g_mlp_ref, w1_ref, w2_ref, g_fin_ref, out_ref, u_ref):
    y = y_ref[...]
    bg, u = _gates(y, g_mix_ref[...], wcin_ref)
    conv = cw_ref[0:1, :] * prev2_ref[...] + cw_ref[1:2, :] * prev1_ref[...] + cw_ref[2:3, :] * u
    u_ref[...] = u
    out_ref[...] = _conv_layer_tail(y, bg, conv, wcout_ref, g_mlp_ref[...],
                                    w1_ref, w2_ref, g_fin_ref[...])


def _sample_conv_mlp(y, prev2, prev1, g_mix, wcin, cw, wcout, g_mlp, w1, w2, g_fin):
    m, d = y.shape
    return pl.pallas_call(
        _sample_conv_mlp_kernel,
        grid=(1,),
        in_specs=[
            _resident((m, d)), _resident((m, d)), _resident((m, d)),
            _resident((1, d)), _resident((d, 3 * d)), _resident((CONV_WIDTH, d)),
            _resident((d, d)), _resident((1, d)),
            _resident((d, D_FF)), _resident((D_FF, d)), _resident((1, d)),
        ],
        out_specs=[pl.BlockSpec((m, d), lambda i: (0, 0)), pl.BlockSpec((m, d), lambda i: (0, 0))],
        out_shape=[jax.ShapeDtypeStruct((m, d), F32), jax.ShapeDtypeStruct((m, d), F32)],
        compiler_params=_params("arbitrary"),
        name="sample_conv_mlp",
    )(y, prev2, prev1, g_mix, wcin, cw, wcout, g_mlp, w1, w2, g_fin)


def _sample_qkv_kernel(x_ref, g_ref, w_ref, qkv_ref):
    qkv_ref[...] = _dot(_rms_norm(x_ref[...], g_ref[...]).astype(BF16), w_ref[...])


def _sample_qkv(x, g, wqkv):
    m, d = x.shape
    return pl.pallas_call(
        _sample_qkv_kernel,
        grid=(1,),
        in_specs=[_resident((m, d)), _resident((1, d)), _resident((d, 3 * d))],
        out_specs=pl.BlockSpec((m, 3 * d), lambda i: (0, 0)),
        out_shape=jax.ShapeDtypeStruct((m, 3 * d), F32),
        compiler_params=_params("arbitrary"),
        name="sample_qkv",
    )(x, g, wqkv)


def _block_key_sum(page_refs):
    nh, hd, ps = page_refs[0].shape[-3:]
    d = nh * hd
    blk = page_refs[0][0, 0].reshape(d, ps)
    for ref in page_refs[1:]:
        blk = blk + ref[0, 0].reshape(d, ps)
    pieces = [jnp.sum(blk[r:r + ps, :].T, axis=0, keepdims=True) for r in range(0, d, ps)]
    return jnp.concatenate(pieces, axis=1)


def _cache_block_sum_kernel(pt_ref, *refs):
    page_refs, out_ref = refs[:-1], refs[-1]
    for j in range(BLOCKS_PER_SUM_STEP):
        out_ref[0, j:j + 1, :] = _block_key_sum(
            page_refs[PAGES_PER_BLOCK * j:PAGES_PER_BLOCK * (j + 1)])


def _cache_block_sums(page_table, cache_t, seq0):
    n_pages = page_table.shape[1]
    nseq = page_table.shape[0] - seq0
    _, _, nh, hd, ps = cache_t.shape
    pages_per_step = BLOCKS_PER_SUM_STEP * PAGES_PER_BLOCK
    n_blocks = n_pages // PAGES_PER_BLOCK

    def page_spec(i):
        return pl.BlockSpec(
            (1, 1, nh, hd, ps),
            lambda b, j, pt: (0, pt[seq0 + b, j * pages_per_step + i], 0, 0, 0))

    return pl.pallas_call(
        _cache_block_sum_kernel,
        grid_spec=pltpu.PrefetchScalarGridSpec(
            num_scalar_prefetch=1,
            grid=(nseq, n_pages // pages_per_step),
            in_specs=[page_spec(i) for i in range(pages_per_step)],
            out_specs=pl.BlockSpec((1, BLOCKS_PER_SUM_STEP, nh * hd), lambda b, j, pt: (b, j, 0)),
        ),
        out_shape=jax.ShapeDtypeStruct((nseq, n_blocks, nh * hd), F32),
        compiler_params=_params("parallel", "arbitrary"),
        name="cache_block_sums",
    )(page_table, *([cache_t] * pages_per_step))


def _sample_gate_kernel(q_ref, csum_ref, blk_ref):
    q = q_ref[0]
    d = q.shape[-1]
    head_row = lax.broadcasted_iota(jnp.int32, (N_HEADS, d), 0)
    head_col = lax.broadcasted_iota(jnp.int32, (N_HEADS, d), 1) // HEAD_DIM
    q_heads = jnp.where(head_row == head_col, q, 0.0).astype(BF16)
    kmean = (csum_ref[0] * (1.0 / BLOCK)).astype(BF16)
    g = _dot_nt(q_heads, kmean)
    nb = g.shape[-1]
    lane = lax.broadcasted_iota(jnp.int32, g.shape, 1)
    out_lane = lax.broadcasted_iota(jnp.int32, blk_ref.shape[1:], 1)
    out = jnp.zeros(blk_ref.shape[1:], jnp.int32)
    for r in range(TOPK):
        mx = jnp.max(g, axis=1, keepdims=True)
        first = jnp.min(jnp.where(g == mx, lane, nb), axis=1, keepdims=True)
        out = jnp.where(out_lane == r, first, out)
        g = jnp.where(lane == first, -jnp.inf, g)
    blk_ref[0] = out


def _sample_gate(q, csum):
    nseq, nb, d = csum.shape
    return pl.pallas_call(
        _sample_gate_kernel,
        grid=(nseq,),
        in_specs=[
            pl.BlockSpec((1, 1, d), lambda b: (b, 0, 0)),
            pl.BlockSpec((1, nb, d), lambda b: (b, 0, 0)),
        ],
        out_specs=pl.BlockSpec((1, N_HEADS, PAGE_SIZE), lambda b: (b, 0, 0)),
        out_shape=jax.ShapeDtypeStruct((nseq, N_HEADS, PAGE_SIZE), jnp.int32),
        compiler_params=_params("parallel"),
        name="sample_gate",
    )(q, csum)


def _sample_attn_kernel(past_len, pt_ref, blk_ref, slope_ref, q_ref, knew_ref, vnew_ref, *refs):
    pages_per_head = TOPK * PAGES_PER_BLOCK
    n_pages = SAMPLE_HEADS_PER_STEP * pages_per_head
    k_refs, v_refs, o_ref = refs[:n_pages], refs[n_pages:2 * n_pages], refs[-1]
    b = pl.program_id(0)
    heads = range(SAMPLE_HEADS_PER_STEP)
    n_keys = pages_per_head * PAGE_SIZE
    lane = lax.broadcasted_iota(jnp.int32, (SAMPLE_ROWS, n_keys), 1)

    def cached(page_refs, hh):
        pages = page_refs[hh * pages_per_head:(hh + 1) * pages_per_head]
        return jnp.concatenate([r[0, 0, 0].astype(BF16) for r in pages], axis=1)

    q = [q_ref[0, hh] for hh in heads]
    logits, self_logit = [], []
    for hh in heads:
        h = pl.program_id(1) * SAMPLE_HEADS_PER_STEP + hh
        pos = jnp.zeros_like(lane)
        for s in range(TOPK):
            first = blk_ref[b, h * TOPK + s] * BLOCK - s * BLOCK
            pos = jnp.where(lane >= s * BLOCK, first + lane, pos)
        slope = slope_ref[hh][:, 0:1]
        logits.append(_dot(q[hh], cached(k_refs, hh)) - slope * (past_len - pos).astype(F32))
        self_logit.append(jnp.sum(q[hh].astype(F32) * knew_ref[0, hh].astype(F32),
                                  axis=-1, keepdims=True))
    probs, p_self, denom = [], [], []
    for hh in heads:
        m = jnp.maximum(self_logit[hh], jnp.max(logits[hh], axis=-1, keepdims=True))
        p = jnp.exp(logits[hh] - m)
        p_self.append(jnp.exp(self_logit[hh] - m))
        denom.append(p_self[hh] + jnp.sum(p, axis=-1, keepdims=True))
        probs.append(p.astype(BF16))
    for hh in heads:
        acc = p_self[hh].astype(BF16).astype(F32) * vnew_ref[0, hh].astype(F32)
        acc = acc + _dot_nt(probs[hh], cached(v_refs, hh))
        o_ref[0, hh] = acc / denom[hh]


def _sample_attn(page_table, blk, slopes, q, k_new, v_new, cache_k_t, cache_v_t, past_len):
    nseq = page_table.shape[0]
    _, _, nh, hd, ps = cache_k_t.shape
    hps = SAMPLE_HEADS_PER_STEP

    def page_spec(hh, s, i):
        def index(b, g, pt, bk):
            h = g * hps + hh
            return (0, pt[b, bk[b, h * TOPK + s] * PAGES_PER_BLOCK + i], h, 0, 0)
        return pl.BlockSpec((1, 1, 1, hd, ps), index)

    page_specs = [page_spec(hh, s, i) for hh in range(hps)
                  for s in range(TOPK) for i in range(PAGES_PER_BLOCK)]
    row_spec = pl.BlockSpec((1, hps, SAMPLE_ROWS, hd), lambda b, g, pt, bk: (b, g, 0, 0))
    n_pages = len(page_specs)
    return pl.pallas_call(
        functools.partial(_sample_attn_kernel, past_len),
        grid_spec=pltpu.PrefetchScalarGridSpec(
            num_scalar_prefetch=2,
            grid=(nseq, nh // hps),
            in_specs=[pl.BlockSpec((hps, 1, ps), lambda b, g, pt, bk: (g, 0, 0)),
                      row_spec, row_spec, row_spec] + page_specs + page_specs,
            out_specs=row_spec,
        ),
        out_shape=jax.ShapeDtypeStruct((nseq, nh, SAMPLE_ROWS, hd), F32),
        compiler_params=_params("parallel", "arbitrary"),
        name="sample_attn",
    )(page_table, blk, slopes, q, k_new, v_new,
      *([cache_k_t] * n_pages), *([cache_v_t] * n_pages))


def _widen_heads(w, width):
    d = w.shape[0]
    w = w.reshape(d, N_HEADS, HEAD_DIM)
    w = jnp.pad(w, ((0, 0), (0, 0), (0, width - HEAD_DIM)))
    return w.reshape(d, N_HEADS * width)


def _alibi_key_bias(slopes):
    rest = slopes[None, :] * jnp.arange(BLOCK, dtype=F32)[:, None]
    terms = []
    for _ in range(BIAS_TERMS):
        bits = lax.bitcast_convert_type(rest, jnp.uint32) & jnp.uint32(0xFFFF0000)
        term = lax.bitcast_convert_type(bits, F32)
        terms.append(term)
        rest = rest - term
    bias = jnp.stack(terms, axis=-1)
    bias = jnp.pad(bias, ((0, 0), (0, 0), (HEAD_DIM, QK_WIDTH - HEAD_DIM - BIAS_TERMS)))
    return bias.reshape(BLOCK, N_HEADS * QK_WIDTH)


def kernel(x_prompt, x_sample, cache_k, cache_v, state_conv, page_table, norm_mix, norm_mlp,
           norm_final, w_qkv, w_attn_out, w_conv_in, conv_w, w_conv_out, w_mlp_in, w_mlp_out):
    b, s, d = x_prompt.shape
    nseq, dec_seq, _ = x_sample.shape
    n_pages = page_table.shape[1]
    past_len = n_pages * PAGE_SIZE
    assert d == D_MODEL and s % TOKEN_TILE == 0 and TOKEN_TILE % BLOCK == 0
    assert dec_seq == 1 and past_len % BLOCK == 0, "the new token must open a fresh key block"
    assert n_pages % (BLOCKS_PER_SUM_STEP * PAGES_PER_BLOCK) == 0
    assert w_qkv.shape[0] == 1 and w_conv_in.shape[0] == 1 and w_mlp_in.shape[0] == 2

    wqkv = w_qkv[0]
    wq = wqkv[:, :d] * QK_SCALE
    wq_t = (wq * LOG2_E).T.astype(BF16)
    wk = wqkv[:, d:2 * d].astype(BF16)
    wk_t = wk.T
    wv_t = wqkv[:, 2 * d:].T.astype(BF16)
    wo = w_attn_out[0].astype(BF16)
    wcin = w_conv_in[0].astype(BF16)
    wcout = w_conv_out[0].astype(BF16)
    w1 = w_mlp_in.astype(BF16)
    w2 = w_mlp_out.astype(BF16)
    g_mix = norm_mix.reshape(2, 1, d)
    g_mlp = norm_mlp.reshape(2, 1, d)
    g_fin = norm_final.reshape(1, d)
    cw = conv_w[0]

    slopes = jnp.exp2(-8.0 * jnp.arange(1, N_HEADS + 1, dtype=F32) / N_HEADS)
    slopes_log2 = slopes * LOG2_E
    slopes_step = jnp.broadcast_to(
        slopes_log2.reshape(N_HEADS // HEADS_PER_STEP, HEADS_PER_STEP, 1),
        (N_HEADS // HEADS_PER_STEP, HEADS_PER_STEP, BLOCK))
    slopes_head = jnp.broadcast_to(slopes.reshape(N_HEADS, 1, 1), (N_HEADS, 1, PAGE_SIZE))

    nb = s // BLOCK
    q_t, k_bf, k_t, v_t, v_t_blk, ksum = _prompt_qkv(
        x_prompt, g_mix[0], wq_t, _widen_heads(wk, QK_WIDTH), wk_t, wv_t,
        _alibi_key_bias(slopes_log2))
    ksum_pos = jnp.pad(ksum.reshape(b, nb, N_HEADS * QK_WIDTH),
                       ((0, 0), (1, ROUTE_ROWS - nb - 1), (0, 0)))
    attn = _prompt_attn(slopes_step, q_t, k_bf, v_t_blk, ksum_pos)
    cache_k_t = jnp.swapaxes(cache_k, -1, -2)
    cache_v_t = jnp.swapaxes(cache_v, -1, -2)
    y, csum_a = _attn_out_mlp_with_block_sums(
        attn.reshape(b * s, d), x_prompt.reshape(b * s, d), wo, g_mlp[0], w1[0], w2[0],
        TOKEN_TILE, page_table, cache_k_t, 0)
    y_prompt, conv_prompt, csum_b = _prompt_conv_mlp(
        y.reshape(b, s, d), g_mix[1], wcin, cw, wcout, g_mlp[1], w1[1], w2[1], g_fin,
        page_table, cache_k_t, csum_a.shape[0])
    csum = jnp.concatenate([csum_a, csum_b])

    xs = x_sample.reshape(nseq, d)
    qkv = _sample_qkv(xs, g_mix[0], jnp.concatenate([wq.astype(BF16), wk, wv_t.T], axis=1))
    q_s, k_s, v_s = qkv[:, :d], qkv[:, d:2 * d], qkv[:, 2 * d:]
    if csum.shape[0] < nseq:
        csum = jnp.concatenate([csum, _cache_block_sums(page_table, cache_k_t, csum.shape[0])])
    csum = csum[:nseq]
    blk = _sample_gate(q_s.reshape(nseq, 1, d), csum)[:, :, :TOPK].reshape(nseq, N_HEADS * TOPK)

    def rows(a, dtype):
        a = a.reshape(nseq, N_HEADS, 1, HEAD_DIM).astype(dtype)
        return jnp.broadcast_to(a, (nseq, N_HEADS, SAMPLE_ROWS, HEAD_DIM))

    o_s = _sample_attn(page_table, blk, slopes_head, rows(q_s, BF16), rows(k_s, BF16),
                       rows(v_s, BF16), cache_k_t, cache_v_t, past_len)
    o_s = o_s[:, :, 0, :].reshape(nseq, d).astype(BF16)
    ys = _attn_out_mlp(o_s, xs, wo, g_mlp[0], w1[0], w2[0], nseq)
    prev = state_conv[0]
    y_sample, u_s = _sample_conv_mlp(ys, prev[:, 0], prev[:, 1], g_mix[1], wcin, cw, wcout,
                                     g_mlp[1], w1[1], w2[1], g_fin)

    k_prompt = jnp.swapaxes(k_t, -1, -2)[None]
    v_prompt = jnp.swapaxes(v_t, -1, -2)[None]
    k_sample = k_s.reshape(1, nseq, N_HEADS, 1, HEAD_DIM)
    v_sample = v_s.reshape(1, nseq, N_HEADS, 1, HEAD_DIM)
    conv_sample = jnp.stack([prev[:, 1], u_s], axis=1)[None]
    return (y_prompt, y_sample.reshape(nseq, 1, d), k_prompt, v_prompt, k_sample, v_sample,
            conv_prompt[None], conv_sample)
```

```python
import functools

import jax
import jax.numpy as jnp
from jax import lax
from jax.experimental import pallas as pl
from jax.experimental.pallas import tpu as pltpu

D_MODEL = 1024
N_HEADS = 16
HEAD_DIM = D_MODEL // N_HEADS
BLOCK = 256
TOPK = 3
PAGE_SIZE = 128
PAGES_PER_BLOCK = BLOCK // PAGE_SIZE
CONV_WIDTH = 3
D_FF = 4 * D_MODEL
NORM_EPS = 1e-5
QK_SCALE = HEAD_DIM ** -0.5
LOG2_E = 1.4426950408889634

F32 = jnp.float32
BF16 = jnp.bfloat16

HEADS_PER_STEP = 4
ROUTE_ROWS = 24
QK_WIDTH = 128
BIAS_TERMS = 3
V_ROWS = 80
TOKEN_TILE = 256
SAMPLE_ROWS = 8
SAMPLE_HEADS_PER_STEP = 4
BLOCKS_PER_SUM_STEP = 8
VMEM_LIMIT_BYTES = 56 * 1024 * 1024

_NT = (((1,), (1,)), ((), ()))


def _dot(a, b):
    return jnp.dot(a, b, preferred_element_type=F32)


def _dot_nt(a, b):
    return lax.dot_general(a, b, _NT, preferred_element_type=F32)


def _rms_norm(x, g):
    return (x * lax.rsqrt(jnp.mean(x * x, axis=-1, keepdims=True) + NORM_EPS)) * g


def _sqrelu_mlp(y, g, w1_ref, w2_ref):
    hn = _rms_norm(y, g).astype(BF16)
    hid = jnp.maximum(_dot(hn, w1_ref[...]), 0.0)
    return _dot((hid * hid).astype(BF16), w2_ref[...])


def _params(*semantics):
    return pltpu.CompilerParams(dimension_semantics=semantics,
                                vmem_limit_bytes=VMEM_LIMIT_BYTES)


def _resident(shape):
    zeros = (0,) * len(shape)
    return pl.BlockSpec(shape, lambda *_: zeros, pipeline_mode=pl.Buffered(1))


def _prompt_qkv_kernel(x_ref, g_ref, wq_t_ref, wk_ref, wk_t_ref, wv_t_ref, kbias_ref,
                       q_t_ref, k_ref, k_t_ref, v_t_ref, v_t_bf_ref, ksum_ref):
    hn = _rms_norm(x_ref[0], g_ref[...]).astype(BF16)
    t = hn.shape[0]
    q_t = _dot_nt(wq_t_ref[...], hn)
    q_pad = QK_WIDTH - HEAD_DIM
    ones_rows = jnp.where(lax.broadcasted_iota(jnp.int32, (q_pad, t), 0) < BIAS_TERMS,
                          1.0, 0.0).astype(BF16)
    for h in range(N_HEADS):
        q_t_ref[0, h * QK_WIDTH:h * QK_WIDTH + HEAD_DIM, :] = (
            q_t[h * HEAD_DIM:(h + 1) * HEAD_DIM].astype(BF16))
        q_t_ref[0, h * QK_WIDTH + HEAD_DIM:(h + 1) * QK_WIDTH, :] = ones_rows

    k = _dot(hn, wk_ref[...])
    heads_per_group = QK_WIDTH // HEAD_DIM
    lane = lax.broadcasted_iota(jnp.int32, (BLOCK, QK_WIDTH), 1)
    for j in range(t // BLOCK):
        rows = slice(j * BLOCK, (j + 1) * BLOCK)
        for h in range(N_HEADS):
            group = k[rows, (h // heads_per_group) * QK_WIDTH:(h // heads_per_group + 1) * QK_WIDTH]
            shift = (h % heads_per_group) * HEAD_DIM
            if shift:
                group = pltpu.roll(group, QK_WIDTH - shift, 1)
            wide = jnp.where(lane < HEAD_DIM, group, 0.0)
            cols = slice(h * QK_WIDTH, (h + 1) * QK_WIDTH)
            ksum_ref[0, j, :, cols] = jnp.sum(wide, axis=0, keepdims=True)
            k_ref[0, rows, cols] = (wide + kbias_ref[:, cols]).astype(BF16)
    k_t_ref[0] = _dot_nt(wk_t_ref[...], hn).reshape(N_HEADS, HEAD_DIM, t)

    v_t = _dot_nt(wv_t_ref[...], hn)
    v_t_ref[0] = v_t.reshape(N_HEADS, HEAD_DIM, t)
    v_pad = V_ROWS - HEAD_DIM
    ones_row = jnp.where(lax.broadcasted_iota(jnp.int32, (v_pad, BLOCK), 0) == 0,
                         1.0, 0.0).astype(BF16)
    for j in range(t // BLOCK):
        for h in range(N_HEADS):
            v_t_bf_ref[0, j, h * V_ROWS:h * V_ROWS + HEAD_DIM, :] = (
                v_t[h * HEAD_DIM:(h + 1) * HEAD_DIM, j * BLOCK:(j + 1) * BLOCK].astype(BF16))
            v_t_bf_ref[0, j, h * V_ROWS + HEAD_DIM:(h + 1) * V_ROWS, :] = ones_row


def _prompt_qkv(x, g, wq_t, wk, wk_t, wv_t, kbias):
    b, s, d = x.shape
    t = TOKEN_TILE
    nb = s // BLOCK
    nbt = t // BLOCK
    dq = N_HEADS * QK_WIDTH
    dv = N_HEADS * V_ROWS
    return pl.pallas_call(
        _prompt_qkv_kernel,
        grid=(b, s // t),
        in_specs=[
            pl.BlockSpec((1, t, d), lambda i, j: (i, j, 0)),
            _resident((1, d)),
            _resident((d, d)), _resident((d, d)), _resident((d, d)), _resident((d, d)),
            _resident((BLOCK, dq)),
        ],
        out_specs=[
            pl.BlockSpec((1, dq, t), lambda i, j: (i, 0, j)),
            pl.BlockSpec((1, t, dq), lambda i, j: (i, j, 0)),
            pl.BlockSpec((1, N_HEADS, HEAD_DIM, t), lambda i, j: (i, 0, 0, j)),
            pl.BlockSpec((1, N_HEADS, HEAD_DIM, t), lambda i, j: (i, 0, 0, j)),
            pl.BlockSpec((1, nbt, dv, BLOCK), lambda i, j: (i, j, 0, 0)),
            pl.BlockSpec((1, nbt, 1, dq), lambda i, j: (i, j, 0, 0)),
        ],
        out_shape=[
            jax.ShapeDtypeStruct((b, dq, s), BF16),
            jax.ShapeDtypeStruct((b, s, dq), BF16),
            jax.ShapeDtypeStruct((b, N_HEADS, HEAD_DIM, s), F32),
            jax.ShapeDtypeStruct((b, N_HEADS, HEAD_DIM, s), F32),
            jax.ShapeDtypeStruct((b, nb, dv, BLOCK), BF16),
            jax.ShapeDtypeStruct((b, nb, 1, dq), F32),
        ],
        compiler_params=_params("parallel", "parallel"),
        name="prompt_qkv",
    )(x, g, wq_t, wk, wk_t, wv_t, kbias)


def _select_routed(gate, c):
    pidx = lax.broadcasted_iota(jnp.int32, gate.shape, 0)
    valid = (pidx >= 1) & (pidx <= c)
    g = jnp.where(valid, gate, -jnp.inf)
    sel = jnp.where(pidx == 0, 1.0, 0.0)
    for _ in range(TOPK):
        mx = jnp.max(g, axis=0, keepdims=True)
        first = jnp.min(jnp.where(g == mx, pidx, gate.shape[0]), axis=0, keepdims=True)
        pick = (pidx == first) & valid
        sel = jnp.where(pick, 1.0, sel)
        g = jnp.where(pick, -jnp.inf, g)
    return sel


def _prompt_attn_kernel(slope_ref, q_t_ref, k_ref, v_t_ref, ksum_ref, o_ref, sel_ref, *scratch):
    heads = range(HEADS_PER_STEP)
    scratch = iter(scratch)
    m_ref = [next(scratch) for _ in heads]
    acc_ref = [next(scratch) for _ in heads]
    alpha_ref = [[next(scratch) for _ in heads] for _ in range(2)]
    z_ref = [[next(scratch) for _ in heads] for _ in range(2)]
    p_ref = [[next(scratch) for _ in heads] for _ in range(2)]
    c = pl.program_id(2)
    nb = v_t_ref.shape[1]
    q = [q_t_ref[0, hh * QK_WIDTH:(hh + 1) * QK_WIDTH, :] for hh in heads]
    slope = [slope_ref[0, hh:hh + 1, :] for hh in heads]

    def block_of(pos):
        if isinstance(pos, int) and pos == 0:
            return c
        return jnp.where(pos == 0, c, jnp.clip(pos - 1, 0, nb - 1))

    def scores(pos, slot, hs=heads, bias=None):
        n = block_of(pos)
        rows = pl.ds(pl.multiple_of(n * BLOCK, BLOCK), BLOCK)
        for hh in hs:
            z = _dot(k_ref[0, rows, hh * QK_WIDTH:(hh + 1) * QK_WIDTH], q[hh])
            z_ref[slot][hh][...] = z if bias is None else z + bias

    def softmax(pos, slot, hs=heads):
        n = block_of(pos)
        for hh in hs:
            z = z_ref[slot][hh][...]
            off = slope[hh] * jnp.full(slope[hh].shape, (c - n) * BLOCK, jnp.int32).astype(F32)
            routed = sel_ref[hh, pl.ds(pos, 1), :] > 0.5
            m_old = m_ref[hh][...]
            m_new = jnp.where(routed, jnp.maximum(m_old, jnp.max(z, axis=0, keepdims=True) - off), m_old)
            m_ref[hh][...] = m_new
            alpha_ref[slot][hh][...] = jnp.exp2(m_old - m_new)
            p_ref[slot][hh][...] = jnp.exp2(z - jnp.where(routed, m_new + off, jnp.inf)).astype(BF16)

    def accumulate(pos, slot, hs=heads):
        n = block_of(pos)
        for hh in hs:
            pv = _dot(v_t_ref[0, n, hh * V_ROWS:(hh + 1) * V_ROWS, :], p_ref[slot][hh][...])
            acc_ref[hh][...] = alpha_ref[slot][hh][...] * acc_ref[hh][...] + pv

    kmean = (ksum_ref[0] * (1.0 / BLOCK)).astype(BF16)
    for hh in heads:
        sel_ref[hh] = _select_routed(_dot(kmean[:, hh * QK_WIDTH:(hh + 1) * QK_WIDTH], q[hh]), c)

    for hh in heads:
        m_ref[hh][...] = jnp.full(m_ref[hh].shape, -1e30, F32)
        acc_ref[hh][...] = jnp.zeros(acc_ref[hh].shape, F32)
        alpha_ref[1][hh][...] = jnp.ones(alpha_ref[1][hh].shape, F32)
        p_ref[1][hh][...] = jnp.zeros(p_ref[1][hh].shape, BF16)
    ki = lax.broadcasted_iota(jnp.int32, (BLOCK, BLOCK), 0)
    qj = lax.broadcasted_iota(jnp.int32, (BLOCK, BLOCK), 1)
    scores(0, 0, bias=jnp.where(ki <= qj, 0.0, -jnp.inf))

    def two_positions(s, carry):
        pos = 2 * s
        accumulate(pos - 1, 1)
        scores(pos + 1, 1)
        softmax(pos, 0)
        accumulate(pos, 0)
        scores(pos + 2, 0)
        softmax(pos + 1, 1)
        return carry

    n_pairs = lax.shift_right_logical(c + 2, 1)
    lax.fori_loop(0, n_pairs, two_positions, 0)
    accumulate(2 * n_pairs - 1, 1)

    outs = []
    for hh in heads:
        acc = acc_ref[hh][...]
        outs.append(acc[:HEAD_DIM] / acc[HEAD_DIM:HEAD_DIM + 1])
    o_ref[0] = jnp.concatenate(outs, axis=0).T.astype(BF16)


def _prompt_attn(slopes, q_t, k, v_t_blk, ksum_pos):
    b, s, dq = k.shape
    nb = s // BLOCK
    n_pos = ksum_pos.shape[1]
    n_steps = N_HEADS // HEADS_PER_STEP
    wq = HEADS_PER_STEP * QK_WIDTH
    wv = HEADS_PER_STEP * V_ROWS
    wo = HEADS_PER_STEP * HEAD_DIM
    assert n_pos >= nb + 2, "softmax reads the routing row of one position past the last"
    tile = (HEADS_PER_STEP, BLOCK, BLOCK)
    return pl.pallas_call(
        _prompt_attn_kernel,
        grid=(b, n_steps, nb),
        in_specs=[
            pl.BlockSpec((1, HEADS_PER_STEP, BLOCK), lambda i, h, c: (h, 0, 0)),
            pl.BlockSpec((1, wq, BLOCK), lambda i, h, c: (i, h, c)),
            pl.BlockSpec((1, s, wq), lambda i, h, c: (i, 0, h)),
            pl.BlockSpec((1, nb, wv, BLOCK), lambda i, h, c: (i, 0, h, 0)),
            pl.BlockSpec((1, n_pos, wq), lambda i, h, c: (i, 0, h)),
        ],
        out_specs=pl.BlockSpec((1, BLOCK, wo), lambda i, h, c: (i, c, h)),
        out_shape=jax.ShapeDtypeStruct((b, s, N_HEADS * HEAD_DIM), BF16),
        scratch_shapes=(
            [pltpu.VMEM((HEADS_PER_STEP, n_pos, BLOCK), F32)]
            + [pltpu.VMEM((1, BLOCK), F32)] * HEADS_PER_STEP
            + [pltpu.VMEM((V_ROWS, BLOCK), F32)] * HEADS_PER_STEP
            + [pltpu.VMEM((1, BLOCK), F32)] * (2 * HEADS_PER_STEP)
            + [pltpu.VMEM((BLOCK, BLOCK), F32)] * (2 * HEADS_PER_STEP)
            + [pltpu.VMEM((BLOCK, BLOCK), BF16)] * (2 * HEADS_PER_STEP)
        ),
        compiler_params=_params("parallel", "parallel", "arbitrary"),
        name="prompt_attn",
    )(slopes, q_t, k, v_t_blk, ksum_pos)


def _attn_out_mlp_kernel(o_ref, x_ref, wo_ref, g_ref, w1_ref, w2_ref, y_ref):
    y = x_ref[...] + _dot(o_ref[...], wo_ref[...])
    y_ref[...] = y + _sqrelu_mlp(y, g_ref[...], w1_ref, w2_ref)


def _attn_out_mlp(o, x, wo, g, w1, w2, tile):
    m, d = x.shape
    return pl.pallas_call(
        _attn_out_mlp_kernel,
        grid=(m // tile,),
        in_specs=[
            pl.BlockSpec((tile, d), lambda i: (i, 0)),
            pl.BlockSpec((tile, d), lambda i: (i, 0)),
            _resident((d, d)), _resident((1, d)),
            _resident((d, D_FF)), _resident((D_FF, d)),
        ],
        out_specs=pl.BlockSpec((tile, d), lambda i: (i, 0)),
        out_shape=jax.ShapeDtypeStruct((m, d), F32),
        compiler_params=_params("parallel"),
        name="attn_out_mlp",
    )(o, x, wo, g, w1, w2)


class _PageStream:
    def __init__(self, page_table, cache_t, n_steps, seq0, step_of):
        n_total, n_pages = page_table.shape
        _, _, nh, hd, ps = cache_t.shape
        self.pages_per_step = BLOCKS_PER_SUM_STEP * PAGES_PER_BLOCK
        steps_per_seq = n_pages // self.pages_per_step
        self.nseq = n_steps // steps_per_seq
        assert self.nseq * steps_per_seq == n_steps
        self.width = nh * hd
        self.blocks = steps_per_seq * BLOCKS_PER_SUM_STEP
        self.operands = [cache_t] * self.pages_per_step

        def page_spec(k):
            def index(*args):
                step, pt = step_of(*args[:-1]), args[-1]
                seq = jnp.minimum(seq0 + step // steps_per_seq, n_total - 1)
                return (0, pt[seq, (step % steps_per_seq) * self.pages_per_step + k], 0, 0, 0)
            return pl.BlockSpec((1, 1, nh, hd, ps), index)

        def out_index(*args):
            step = step_of(*args[:-1])
            return (step // steps_per_seq, step % steps_per_seq, 0, 0)

        self.in_specs = [page_spec(k) for k in range(self.pages_per_step)]
        self.out_spec = pl.BlockSpec((1, 1, BLOCKS_PER_SUM_STEP, self.width), out_index)
        self.out_shape = jax.ShapeDtypeStruct(
            (self.nseq, steps_per_seq, BLOCKS_PER_SUM_STEP, self.width), F32)

    def finish(self, csum):
        return csum.reshape(self.nseq, self.blocks, self.width)


def _store_block_key_sums(page_refs, csum_ref):
    for j in range(csum_ref.shape[2]):
        csum_ref[0, 0, j:j + 1, :] = _block_key_sum(
            page_refs[PAGES_PER_BLOCK * j:PAGES_PER_BLOCK * (j + 1)])


def _attn_out_mlp_sums_kernel(pt_ref, o_ref, x_ref, wo_ref, g_ref, w1_ref, w2_ref, *refs):
    page_refs, y_ref, csum_ref = refs[:-2], refs[-2], refs[-1]
    _store_block_key_sums(page_refs, csum_ref)
    y = x_ref[...] + _dot(o_ref[...], wo_ref[...])
    y_ref[...] = y + _sqrelu_mlp(y, g_ref[...], w1_ref, w2_ref)


def _attn_out_mlp_with_block_sums(o, x, wo, g, w1, w2, tile, page_table, cache_t, seq0):
    m, d = x.shape
    n_steps = m // tile
    stream = _PageStream(page_table, cache_t, n_steps, seq0, lambda i: i)
    y, csum = pl.pallas_call(
        _attn_out_mlp_sums_kernel,
        grid_spec=pltpu.PrefetchScalarGridSpec(
            num_scalar_prefetch=1,
            grid=(n_steps,),
            in_specs=[
                pl.BlockSpec((tile, d), lambda i, pt: (i, 0)),
                pl.BlockSpec((tile, d), lambda i, pt: (i, 0)),
                _resident((d, d)), _resident((1, d)),
                _resident((d, D_FF)), _resident((D_FF, d)),
            ] + stream.in_specs,
            out_specs=[pl.BlockSpec((tile, d), lambda i, pt: (i, 0)), stream.out_spec],
        ),
        out_shape=[jax.ShapeDtypeStruct((m, d), F32), stream.out_shape],
        compiler_params=_params("parallel"),
        name="attn_out_mlp_sums",
    )(page_table, o, x, wo, g, w1, w2, *stream.operands)
    return y, stream.finish(csum)


def _conv_layer_tail(y, bg, conv, wcout_ref, g_mlp, w1_ref, w2_ref, g_final):
    y = y + _dot((bg * conv).astype(BF16), wcout_ref[...])
    y = y + _sqrelu_mlp(y, g_mlp, w1_ref, w2_ref)
    return _rms_norm(y, g_final)


def _gates(y, g_mix, wcin_ref):
    d = y.shape[-1]
    z = _dot(_rms_norm(y, g_mix).astype(BF16), wcin_ref[...])
    return z[:, :d], z[:, d:2 * d] * z[:, 2 * d:]


def _prompt_conv_mlp_kernel(pt_ref, y_ref, g_mix_ref, wcin_ref, cw_ref, wcout_ref, g_mlp_ref,
                            w1_ref, w2_ref, g_fin_ref, *refs):
    page_refs, (out_ref, state_ref, csum_ref, carry_ref) = refs[:-4], refs[-4:]
    j = pl.program_id(1)

    @pl.when(j == 0)
    def _():
        carry_ref[...] = jnp.zeros_like(carry_ref)

    _store_block_key_sums(page_refs, csum_ref)
    y = y_ref[0]
    t = y.shape[0]
    bg, u = _gates(y, g_mix_ref[...], wcin_ref)
    r = lax.broadcasted_iota(jnp.int32, u.shape, 0)
    prev2 = carry_ref[0:1, :]
    prev1 = carry_ref[1:2, :]
    u1 = jnp.where(r == 0, prev1, pltpu.roll(u, 1, 0))
    u2 = jnp.where(r == 0, prev2, jnp.where(r == 1, prev1, pltpu.roll(u, 2, 0)))
    conv = cw_ref[0:1, :] * u2 + cw_ref[1:2, :] * u1 + cw_ref[2:3, :] * u
    carry_ref[...] = u[t - (CONV_WIDTH - 1):, :]
    out_ref[0] = _conv_layer_tail(y, bg, conv, wcout_ref, g_mlp_ref[...],
                                  w1_ref, w2_ref, g_fin_ref[...])

    @pl.when(j == pl.num_programs(1) - 1)
    def _():
        state_ref[0] = u[t - (CONV_WIDTH - 1):, :]


def _prompt_conv_mlp(y, g_mix, wcin, cw, wcout, g_mlp, w1, w2, g_fin, page_table, cache_t, seq0):
    b, s, d = y.shape
    t = TOKEN_TILE
    steps = s // t
    stream = _PageStream(page_table, cache_t, b * steps, seq0, lambda i, j: i * steps + j)
    out, state, csum = pl.pallas_call(
        _prompt_conv_mlp_kernel,
        grid_spec=pltpu.PrefetchScalarGridSpec(
            num_scalar_prefetch=1,
            grid=(b, steps),
            in_specs=[
                pl.BlockSpec((1, t, d), lambda i, j, pt: (i, j, 0)),
                _resident((1, d)), _resident((d, 3 * d)), _resident((CONV_WIDTH, d)),
                _resident((d, d)), _resident((1, d)),
                _resident((d, D_FF)), _resident((D_FF, d)), _resident((1, d)),
            ] + stream.in_specs,
            out_specs=[
                pl.BlockSpec((1, t, d), lambda i, j, pt: (i, j, 0)),
                pl.BlockSpec((1, CONV_WIDTH - 1, d), lambda i, j, pt: (i, 0, 0)),
                stream.out_spec,
            ],
            scratch_shapes=[pltpu.VMEM((CONV_WIDTH - 1, d), F32)],
        ),
        out_shape=[
            jax.ShapeDtypeStruct((b, s, d), F32),
            jax.ShapeDtypeStruct((b, CONV_WIDTH - 1, d), F32),
            stream.out_shape,
        ],
        compiler_params=_params("parallel", "arbitrary"),
        name="prompt_conv_mlp",
    )(page_table, y, g_mix, wcin, cw, wcout, g_mlp, w1, w2, g_fin, *stream.operands)
    return out, state, stream.finish(csum)


def _sample_conv_mlp_kernel(y_ref, prev2_ref, prev1_ref, g_mix_ref, wcin_ref, cw_ref,
                            wcout_ref, g_mlp_ref, w1_ref, w2_ref, g_fin_ref, out_ref, u_ref):
    y = y_ref[...]
    bg, u = _gates(y, g_mix_ref[...], wcin_ref)
    conv = cw_ref[0:1, :] * prev2_ref[...] + cw_ref[1:2, :] * prev1_ref[...] + cw_ref[2:3, :] * u
    u_ref[...] = u
    out_ref[...] = _conv_layer_tail(y, bg, conv, wcout_ref, g_mlp_ref[...],
                                    w1_ref, w2_ref, g_fin_ref[...])


def _sample_conv_mlp(y, prev2, prev1, g_mix, wcin, cw, wcout, g_mlp, w1, w2, g_fin):
    m, d = y.shape
    return pl.pallas_call(
        _sample_conv_mlp_kernel,
        grid=(1,),
        in_specs=[
            _resident((m, d)), _resident((m, d)), _resident((m, d)),
            _resident((1, d)), _resident((d, 3 * d)), _resident((CONV_WIDTH, d)),
            _resident((d, d)), _resident((1, d)),
            _resident((d, D_FF)), _resident((D_FF, d)), _resident((1, d)),
        ],
        out_specs=[pl.BlockSpec((m, d), lambda i: (0, 0)), pl.BlockSpec((m, d), lambda i: (0, 0))],
        out_shape=[jax.ShapeDtypeStruct((m, d), F32), jax.ShapeDtypeStruct((m, d), F32)],
        compiler_params=_params("arbitrary"),
        name="sample_conv_mlp",
    )(y, prev2, prev1, g_mix, wcin, cw, wcout, g_mlp, w1, w2, g_fin)


def _sample_qkv_kernel(x_ref, g_ref, w_ref, qkv_ref):
    qkv_ref[...] = _dot(_rms_norm(x_ref[...], g_ref[...]).astype(BF16), w_ref[...])


def _sample_qkv(x, g, wqkv):
    m, d = x.shape
    return pl.pallas_call(
        _sample_qkv_kernel,
        grid=(1,),
        in_specs=[_resident((m, d)), _resident((1, d)), _resident((d, 3 * d))],
        out_specs=pl.BlockSpec((m, 3 * d), lambda i: (0, 0)),
        out_shape=jax.ShapeDtypeStruct((m, 3 * d), F32),
        compiler_params=_params("arbitrary"),
        name="sample_qkv",
    )(x, g, wqkv)


def _block_key_sum(page_refs):
    nh, hd, ps = page_refs[0].shape[-3:]
    d = nh * hd
    blk = page_refs[0][0, 0].reshape(d, ps)
    for ref in page_refs[1:]:
        blk = blk + ref[0, 0].reshape(d, ps)
    pieces = [jnp.sum(blk[r:r + ps, :].T, axis=0, keepdims=True) for r in range(0, d, ps)]
    return jnp.concatenate(pieces, axis=1)


def _cache_block_sum_kernel(pt_ref, *refs):
    page_refs, out_ref = refs[:-1], refs[-1]
    for j in range(BLOCKS_PER_SUM_STEP):
        out_ref[0, j:j + 1, :] = _block_key_sum(
            page_refs[PAGES_PER_BLOCK * j:PAGES_PER_BLOCK * (j + 1)])


def _cache_block_sums(page_table, cache_t, seq0):
    n_pages = page_table.shape[1]
    nseq = page_table.shape[0] - seq0
    _, _, nh, hd, ps = cache_t.shape
    pages_per_step = BLOCKS_PER_SUM_STEP * PAGES_PER_BLOCK
    n_blocks = n_pages // PAGES_PER_BLOCK

    def page_spec(i):
        return pl.BlockSpec(
            (1, 1, nh, hd, ps),
            lambda b, j, pt: (0, pt[seq0 + b, j * pages_per_step + i], 0, 0, 0))

    return pl.pallas_call(
        _cache_block_sum_kernel,
        grid_spec=pltpu.PrefetchScalarGridSpec(
            num_scalar_prefetch=1,
            grid=(nseq, n_pages // pages_per_step),
            in_specs=[page_spec(i) for i in range(pages_per_step)],
            out_specs=pl.BlockSpec((1, BLOCKS_PER_SUM_STEP, nh * hd), lambda b, j, pt: (b, j, 0)),
        ),
        out_shape=jax.ShapeDtypeStruct((nseq, n_blocks, nh * hd), F32),
        compiler_params=_params("parallel", "arbitrary"),
        name="cache_block_sums",
    )(page_table, *([cache_t] * pages_per_step))


def _sample_gate_kernel(q_ref, csum_ref, blk_ref):
    prod = (csum_ref[0] * (1.0 / BLOCK)) * q_ref[0]
    nb = prod.shape[0]
    lane = lax.broadcasted_iota(jnp.int32, (nb, PAGE_SIZE), 1)
    row = lax.broadcasted_iota(jnp.int32, (nb, PAGE_SIZE), 0)
    g = jnp.full((nb, PAGE_SIZE), -jnp.inf, F32)
    for h in range(N_HEADS):
        col = jnp.sum(prod[:, h * HEAD_DIM:(h + 1) * HEAD_DIM], axis=1, keepdims=True)
        g = jnp.where(lane == h, col, g)
    out_row = lax.broadcasted_iota(jnp.int32, blk_ref.shape[1:], 0)
    out = jnp.zeros(blk_ref.shape[1:], jnp.int32)
    for r in range(TOPK):
        mx = jnp.max(g, axis=0, keepdims=True)
        first = jnp.min(jnp.where(g == mx, row, nb), axis=0, keepdims=True)
        out = jnp.where(out_row == r, first, out)
        g = jnp.where(row == first, -jnp.inf, g)
    blk_ref[0] = out


def _sample_gate(q, csum):
    nseq, nb, d = csum.shape
    return pl.pallas_call(
        _sample_gate_kernel,
        grid=(nseq,),
        in_specs=[
            pl.BlockSpec((1, 1, d), lambda b: (b, 0, 0)),
            pl.BlockSpec((1, nb, d), lambda b: (b, 0, 0)),
        ],
        out_specs=pl.BlockSpec((1, SAMPLE_ROWS, PAGE_SIZE), lambda b: (b, 0, 0)),
        out_shape=jax.ShapeDtypeStruct((nseq, SAMPLE_ROWS, PAGE_SIZE), jnp.int32),
        compiler_params=_params("parallel"),
        name="sample_gate",
    )(q, csum)


def _sample_attn_kernel(past_len, pt_ref, blk_ref, slope_ref, q_ref, knew_ref, vnew_ref, *refs):
    pages_per_head = TOPK * PAGES_PER_BLOCK
    n_pages = SAMPLE_HEADS_PER_STEP * pages_per_head
    k_refs, v_refs, o_ref = refs[:n_pages], refs[n_pages:2 * n_pages], refs[-1]
    b = pl.program_id(0)
    heads = range(SAMPLE_HEADS_PER_STEP)
    n_keys = pages_per_head * PAGE_SIZE
    lane = lax.broadcasted_iota(jnp.int32, (SAMPLE_ROWS, n_keys), 1)

    def cached(page_refs, hh):
        pages = page_refs[hh * pages_per_head:(hh + 1) * pages_per_head]
        return jnp.concatenate([r[0, 0, 0].astype(BF16) for r in pages], axis=1)

    q = [q_ref[0, hh] for hh in heads]
    logits, self_logit = [], []
    for hh in heads:
        h = pl.program_id(1) * SAMPLE_HEADS_PER_STEP + hh
        pos = jnp.zeros_like(lane)
        for s in range(TOPK):
            first = blk_ref[b, h * TOPK + s] * BLOCK - s * BLOCK
            pos = jnp.where(lane >= s * BLOCK, first + lane, pos)
        slope = slope_ref[hh][:, 0:1]
        logits.append(_dot(q[hh], cached(k_refs, hh)) - slope * (past_len - pos).astype(F32))
        self_logit.append(jnp.sum(q[hh].astype(F32) * knew_ref[0, hh].astype(F32),
                                  axis=-1, keepdims=True))
    probs, p_self, denom = [], [], []
    for hh in heads:
        m = jnp.maximum(self_logit[hh], jnp.max(logits[hh], axis=-1, keepdims=True))
        p = jnp.exp(logits[hh] - m)
        p_self.append(jnp.exp(self_logit[hh] - m))
        denom.append(p_self[hh] + jnp.sum(p, axis=-1, keepdims=True))
        probs.append(p.astype(BF16))
    for hh in heads:
        acc = p_self[hh].astype(BF16).astype(F32) * vnew_ref[0, hh].astype(F32)
        acc = acc + _dot_nt(probs[hh], cached(v_refs, hh))
        o_ref[0, hh] = acc / denom[hh]


def _sample_attn(page_table, blk, slopes, q, k_new, v_new, cache_k_t, cache_v_t, past_len):
    nseq = page_table.shape[0]
    _, _, nh, hd, ps = cache_k_t.shape
    hps = SAMPLE_HEADS_PER_STEP

    def page_spec(hh, s, i):
        def index(b, g, pt, bk):
            h = g * hps + hh
            return (0, pt[b, bk[b, h * TOPK + s] * PAGES_PER_BLOCK + i], h, 0, 0)
        return pl.BlockSpec((1, 1, 1, hd, ps), index)

    page_specs = [page_spec(hh, s, i) for hh in range(hps)
                  for s in range(TOPK) for i in range(PAGES_PER_BLOCK)]
    row_spec = pl.BlockSpec((1, hps, SAMPLE_ROWS, hd), lambda b, g, pt, bk: (b, g, 0, 0))
    n_pages = len(page_specs)
    return pl.pallas_call(
        functools.partial(_sample_attn_kernel, past_len),
        grid_spec=pltpu.PrefetchScalarGridSpec(
            num_scalar_prefetch=2,
            grid=(nseq, nh // hps),
            in_specs=[pl.BlockSpec((hps, 1, ps), lambda b, g, pt, bk: (g, 0, 0)),
                      row_spec, row_spec, row_spec] + page_specs + page_specs,
            out_specs=row_spec,
        ),
        out_shape=jax.ShapeDtypeStruct((nseq, nh, SAMPLE_ROWS, hd), F32),
        compiler_params=_params("parallel", "arbitrary"),
        name="sample_attn",
    )(page_table, blk, slopes, q, k_new, v_new,
      *([cache_k_t] * n_pages), *([cache_v_t] * n_pages))


def _alibi_key_bias(slopes):
    rest = slopes[None, :] * jnp.arange(BLOCK, dtype=F32)[:, None]
    terms = []
    for _ in range(BIAS_TERMS):
        bits = lax.bitcast_convert_type(rest, jnp.uint32) & jnp.uint32(0xFFFF0000)
        term = lax.bitcast_convert_type(bits, F32)
        terms.append(term)
        rest = rest - term
    bias = jnp.stack(terms, axis=-1)
    bias = jnp.pad(bias, ((0, 0), (0, 0), (HEAD_DIM, QK_WIDTH - HEAD_DIM - BIAS_TERMS)))
    return bias.reshape(BLOCK, N_HEADS * QK_WIDTH)


def kernel(x_prompt, x_sample, cache_k, cache_v, state_conv, page_table, norm_mix, norm_mlp,
           norm_final, w_qkv, w_attn_out, w_conv_in, conv_w, w_conv_out, w_mlp_in, w_mlp_out):
    b, s, d = x_prompt.shape
    nseq, dec_seq, _ = x_sample.shape
    n_pages = page_table.shape[1]
    past_len = n_pages * PAGE_SIZE
    assert d == D_MODEL and s % TOKEN_TILE == 0 and TOKEN_TILE % BLOCK == 0
    assert dec_seq == 1 and past_len % BLOCK == 0, "the new token must open a fresh key block"
    assert n_pages % (BLOCKS_PER_SUM_STEP * PAGES_PER_BLOCK) == 0
    assert w_qkv.shape[0] == 1 and w_conv_in.shape[0] == 1 and w_mlp_in.shape[0] == 2

    wqkv = w_qkv[0]
    wq = wqkv[:, :d] * QK_SCALE
    wq_t = (wq * LOG2_E).T.astype(BF16)
    wk = wqkv[:, d:2 * d].astype(BF16)
    wk_t = wk.T
    wv_t = wqkv[:, 2 * d:].T.astype(BF16)
    wo = w_attn_out[0].astype(BF16)
    wcin = w_conv_in[0].astype(BF16)
    wcout = w_conv_out[0].astype(BF16)
    w1 = w_mlp_in.astype(BF16)
    w2 = w_mlp_out.astype(BF16)
    g_mix = norm_mix.reshape(2, 1, d)
    g_mlp = norm_mlp.reshape(2, 1, d)
    g_fin = norm_final.reshape(1, d)
    cw = conv_w[0]

    slopes = jnp.exp2(-8.0 * jnp.arange(1, N_HEADS + 1, dtype=F32) / N_HEADS)
    slopes_log2 = slopes * LOG2_E
    slopes_step = jnp.broadcast_to(
        slopes_log2.reshape(N_HEADS // HEADS_PER_STEP, HEADS_PER_STEP, 1),
        (N_HEADS // HEADS_PER_STEP, HEADS_PER_STEP, BLOCK))
    slopes_head = jnp.broadcast_to(slopes.reshape(N_HEADS, 1, 1), (N_HEADS, 1, PAGE_SIZE))

    nb = s // BLOCK
    q_t, k_bf, k_t, v_t, v_t_blk, ksum = _prompt_qkv(
        x_prompt, g_mix[0], wq_t, wk, wk_t, wv_t,
        _alibi_key_bias(slopes_log2))
    ksum_pos = jnp.pad(ksum.reshape(b, nb, N_HEADS * QK_WIDTH),
                       ((0, 0), (1, ROUTE_ROWS - nb - 1), (0, 0)))
    attn = _prompt_attn(slopes_step, q_t, k_bf, v_t_blk, ksum_pos)
    cache_k_t = jnp.swapaxes(cache_k, -1, -2)
    cache_v_t = jnp.swapaxes(cache_v, -1, -2)
    y, csum_a = _attn_out_mlp_with_block_sums(
        attn.reshape(b * s, d), x_prompt.reshape(b * s, d), wo, g_mlp[0], w1[0], w2[0],
        TOKEN_TILE, page_table, cache_k_t, 0)
    y_prompt, conv_prompt, csum_b = _prompt_conv_mlp(
        y.reshape(b, s, d), g_mix[1], wcin, cw, wcout, g_mlp[1], w1[1], w2[1], g_fin,
        page_table, cache_k_t, csum_a.shape[0])
    csum = jnp.concatenate([csum_a, csum_b])

    xs = x_sample.reshape(nseq, d)
    qkv = _sample_qkv(xs, g_mix[0], jnp.concatenate([wq.astype(BF16), wk, wv_t.T], axis=1))
    q_s, k_s, v_s = qkv[:, :d], qkv[:, d:2 * d], qkv[:, 2 * d:]
    if csum.shape[0] < nseq:
        csum = jnp.concatenate([csum, _cache_block_sums(page_table, cache_k_t, csum.shape[0])])
    csum = csum[:nseq]
    blk = _sample_gate(q_s.reshape(nseq, 1, d), csum)[:, :TOPK, :N_HEADS]
    blk = jnp.swapaxes(blk, 1, 2).reshape(nseq, N_HEADS * TOPK)

    def rows(a, dtype):
        a = a.reshape(nseq, N_HEADS, 1, HEAD_DIM).astype(dtype)
        return jnp.broadcast_to(a, (nseq, N_HEADS, SAMPLE_ROWS, HEAD_DIM))

    o_s = _sample_attn(page_table, blk, slopes_head, rows(q_s, BF16), rows(k_s, BF16),
                       rows(v_s, BF16), cache_k_t, cache_v_t, past_len)
    o_s = o_s[:, :, 0, :].reshape(nseq, d).astype(BF16)
    ys = _attn_out_mlp(o_s, xs, wo, g_mlp[0], w1[0], w2[0], nseq)
    prev = state_conv[0]
    y_sample, u_s = _sample_conv_mlp(ys, prev[:, 0], prev[:, 1], g_mix[1], wcin, cw, wcout,
                                     g_mlp[1], w1[1], w2[1], g_fin)

    k_prompt = jnp.swapaxes(k_t, -1, -2)[None]
    v_prompt = jnp.swapaxes(v_t, -1, -2)[None]
    k_sample = k_s.reshape(1, nseq, N_HEADS, 1, HEAD_DIM)
    v_sample = v_s.reshape(1, nseq, N_HEADS, 1, HEAD_DIM)
    conv_sample = jnp.stack([prev[:, 1], u_s], axis=1)[None]
    return (y_prompt, y_sample.reshape(nseq, 1, d), k_prompt, v_prompt, k_sample, v_sample,
            conv_prompt[None], conv_sample)
```
